```python
import jax, jax.numpy as jnp
from jax import lax
import numpy as np

D_MODEL = 1024
BATCH = 16
SEQ = 2048
DEPTH = 2

N_EVEN = (DEPTH + 1) // 2
N_ODD = DEPTH // 2
EPS = 1e-6
CONV_WIDTH = 4

LRU_WIDTH = D_MODEL // 2
LRU_BLOCKS = 8
LRU_BLOCK = LRU_WIDTH // LRU_BLOCKS
LRU_C = 8.0
RET_HEADS = 4
RET_DK = 128
RET_DV = 128
RET_CHUNK = 64
ROPE_BASE = 10000.0
HG_HEADS = 4
HG_DK = 128
HG_DV = 128
HG_CHUNK = 32
GD_HEADS = 4
GD_DK = 128
GD_DV = 128
GD_CHUNK = 64
MLP_HIDDEN = 4 * D_MODEL

EVEN_SIZES = (LRU_WIDTH, LRU_WIDTH, RET_HEADS * RET_DK, RET_HEADS * RET_DK, RET_HEADS * RET_DV, RET_HEADS * RET_DV)
EVEN_IN = sum(EVEN_SIZES)
EVEN_MIX = LRU_WIDTH + RET_HEADS * RET_DV
ODD_SIZES = (HG_HEADS * HG_DK, HG_HEADS * HG_DK, HG_HEADS * HG_DV, HG_HEADS * HG_DV,
             GD_HEADS * GD_DK, GD_HEADS * GD_DK, GD_HEADS * GD_DV, GD_HEADS * GD_DV, GD_HEADS, GD_HEADS)
ODD_IN = sum(ODD_SIZES)
ODD_MIX = HG_HEADS * HG_DV + GD_HEADS * GD_DV

kernel_name = 'hybrid_rglru_retention_hgrn2_gdn_adaln'


def _split(t, sizes):
    return jnp.split(t, np.cumsum(sizes)[:-1].tolist(), axis=-1)


def rmsnorm(x, w):
    xf = x.astype(jnp.float32)
    y = xf * lax.rsqrt(jnp.mean(xf * xf, axis=-1, keepdims=True) + EPS)
    return (y * w.astype(jnp.float32)).astype(x.dtype)


def modulate(h, shift, scale):
    return h * (1.0 + scale[:, None, :]) + shift[:, None, :]


def causal_depthwise_conv(x, w, b=None):
    C = x.shape[-1]
    y = lax.conv_general_dilated(x, w[:, None, :].astype(x.dtype), window_strides=(1,),
                                 padding=[(w.shape[0] - 1, 0)],
                                 dimension_numbers=('NWC', 'WIO', 'NWC'), feature_group_count=C)
    if b is not None:
        y = y + b.astype(x.dtype)
    return y


def rotary(x, positions):
    half = x.shape[-1] // 2
    inv = ROPE_BASE ** (-jnp.arange(half, dtype=jnp.float32) / half)
    ang = positions.astype(jnp.float32)[..., None] * inv
    cos = jnp.cos(ang)[:, :, None, :]
    sin = jnp.sin(ang)[:, :, None, :]
    x1, x2 = x[..., :half], x[..., half:]
    return jnp.concatenate([x1 * cos - x2 * sin, x1 * sin + x2 * cos], axis=-1)


def l2norm(x):
    return x * lax.rsqrt(jnp.sum(x * x, axis=-1, keepdims=True) + EPS)


def _to_chunks(t, C):
    B, S, H = t.shape[:3]
    t = t.reshape((B, S // C, C, H) + t.shape[3:])
    return jnp.moveaxis(t, 3, 1)


def _from_chunks(t):
    B, H, N, C = t.shape[:4]
    return jnp.moveaxis(t, 1, 3).reshape((B, N * C, H) + t.shape[4:])


def _scan_states(chunk_decay, kv):
    def step(S, inp):
        d, u = inp
        return d[..., None] * S + u, S
    S0 = jnp.zeros_like(kv[:, :, 0])
    _, states = lax.scan(step, S0, (jnp.moveaxis(chunk_decay, 2, 0), jnp.moveaxis(kv, 2, 0)))
    return jnp.moveaxis(states, 0, 2)


def rg_lru(x, w_a, b_a, w_x, b_x, lam):
    B, S, W = x.shape
    xb = x.reshape(B, S, LRU_BLOCKS, LRU_BLOCK)
    r = jax.nn.sigmoid(jnp.einsum('bsni,nij->bsnj', xb, w_a.astype(jnp.float32)).reshape(B, S, W) + b_a.astype(jnp.float32))
    i = jax.nn.sigmoid(jnp.einsum('bsni,nij->bsnj', xb, w_x.astype(jnp.float32)).reshape(B, S, W) + b_x.astype(jnp.float32))
    log_a = -LRU_C * r * jax.nn.softplus(-lam.astype(jnp.float32))
    a = jnp.exp(log_a)
    u = jnp.sqrt(-jnp.expm1(2.0 * log_a)) * (i * x)

    def combine(lhs, rhs):
        a1, b1 = lhs
        a2, b2 = rhs
        return a1 * a2, a2 * b1 + b2
    _, h = lax.associative_scan(combine, (a, u), axis=1)
    return h


def retention_chunkwise(q, k, v):
    H = q.shape[2]
    C = RET_CHUNK
    log_gamma = jnp.log1p(-jnp.power(2.0, -5.0 - jnp.arange(H, dtype=jnp.float32)))
    qc, kc, vc = _to_chunks(q, C), _to_chunks(k, C), _to_chunks(v, C)
    B, _, N = qc.shape[:3]
    idx = jnp.arange(C, dtype=jnp.float32)
    rel = idx[:, None] - idx[None, :]
    dmat = jnp.where(rel >= 0, jnp.exp(log_gamma[:, None, None] * jnp.maximum(rel, 0.0)), 0.0)
    scores = jnp.einsum('bhncd,bhnsd->bhncs', qc, kc) * dmat[None, :, None]
    o = jnp.einsum('bhncs,bhnse->bhnce', scores, vc)
    k_dec = jnp.exp(log_gamma[:, None] * (C - 1 - idx))
    kv = jnp.einsum('bhnsd,bhnse->bhnde', kc * k_dec[None, :, None, :, None], vc)
    chunk_decay = jnp.broadcast_to(jnp.exp(log_gamma * C)[None, :, None, None], (B, H, N, qc.shape[-1]))
    states = _scan_states(chunk_decay, kv)
    q_dec = jnp.exp(log_gamma[:, None] * (idx + 1.0))
    o = o + jnp.einsum('bhncd,bhnde->bhnce', qc * q_dec[None, :, None, :, None], states)
    return _from_chunks(o)


def hgrn2_chunkwise(q, k, v, log_f):
    C = HG_CHUNK
    mid = C // 2
    qc, kc, vc, lfc = _to_chunks(q, C), _to_chunks(k, C), _to_chunks(v, C), _to_chunks(log_f, C)
    b = jnp.cumsum(lfc, axis=3)
    b_mid = b[:, :, :, mid:mid + 1]
    qi = qc * jnp.exp(b - b_mid)
    ki = kc * jnp.exp(b_mid - b)
    causal = jnp.tril(jnp.ones((C, C), dtype=bool))
    scores = jnp.where(causal, jnp.einsum('bhncd,bhnsd->bhncs', qi, ki), 0.0)
    o = jnp.einsum('bhncs,bhnse->bhnce', scores, vc)
    b_last = b[:, :, :, -1:]
    kv = jnp.einsum('bhnsd,bhnse->bhnde', kc * jnp.exp(b_last - b), vc)
    states = _scan_states(jnp.exp(b_last[:, :, :, 0]), kv)
    o = o + jnp.einsum('bhncd,bhnde->bhnce', qc * jnp.exp(b), states)
    return _from_chunks(o)


def gated_delta_chunkwise(q, k, v, g, beta):
    C = GD_CHUNK
    qc, kc, vc = _to_chunks(q, C), _to_chunks(k, C), _to_chunks(v, C)
    gc, bc = _to_chunks(g, C), _to_chunks(beta, C)
    B, H, N = qc.shape[:3]
    dk, dv = qc.shape[-1], vc.shape[-1]
    gcum = jnp.cumsum(gc, axis=-1)
    incl = jnp.tril(jnp.ones((C, C), dtype=bool))
    strict = jnp.tril(jnp.ones((C, C), dtype=bool), k=-1)
    diff = gcum[..., :, None] - gcum[..., None, :]
    decay = jnp.where(incl, jnp.exp(jnp.where(incl, diff, 0.0)), 0.0)
    k_beta = kc * bc[..., None]
    v_beta = vc * bc[..., None]
    A = jnp.where(strict, jnp.einsum('bhncd,bhnsd->bhncs', k_beta, kc) * decay, 0.0)
    rhs = jnp.concatenate([v_beta, k_beta * jnp.exp(gcum)[..., None]], axis=-1)
    sol = lax.linalg.triangular_solve(jnp.eye(C, dtype=A.dtype) + A, rhs, left_side=True,
                                      lower=True, unit_diagonal=True)
    u, w = sol[..., :dv], sol[..., dv:]
    qk = jnp.where(incl, jnp.einsum('bhncd,bhnsd->bhncs', qc, kc) * decay, 0.0)
    q_g = qc * jnp.exp(gcum)[..., None]
    k_g = kc * jnp.exp(gcum[..., -1:] - gcum)[..., None]
    last = jnp.exp(gcum[..., -1])

    def step(S, inp):
        u_n, w_n, qk_n, q_n, k_n, d_n = inp
        v_new = u_n - jnp.einsum('bhcd,bhde->bhce', w_n, S)
        o_n = jnp.einsum('bhcd,bhde->bhce', q_n, S) + jnp.einsum('bhcs,bhse->bhce', qk_n, v_new)
        S = d_n[..., None, None] * S + jnp.einsum('bhsd,bhse->bhde', k_n, v_new)
        return S, o_n
    xs = tuple(jnp.moveaxis(t, 2, 0) for t in (u, w, qk, q_g, k_g, last))
    S0 = jnp.zeros((B, H, dk, dv), dtype=qc.dtype)
    _, o = lax.scan(step, S0, xs)
    return _from_chunks(jnp.moveaxis(o, 0, 2))


def even_mixer(h, positions, w_in, conv_w, conv_b, w_a, b_a, w_x, b_x, lam, w_out):
    B, S, _ = h.shape
    proj = (h @ w_in).astype(jnp.float32)
    xr, yr, q, k, v, g = _split(proj, EVEN_SIZES)
    xr = causal_depthwise_conv(xr, conv_w, conv_b)
    lru = rg_lru(xr, w_a, b_a, w_x, b_x, lam) * jax.nn.gelu(yr, approximate=True)
    q = rotary(q.reshape(B, S, RET_HEADS, RET_DK), positions)
    k = rotary(k.reshape(B, S, RET_HEADS, RET_DK), positions) * (RET_DK ** -0.5)
    o = retention_chunkwise(q, k, v.reshape(B, S, RET_HEADS, RET_DV))
    mu = jnp.mean(o, axis=-1, keepdims=True)
    var = jnp.mean(jnp.square(o - mu), axis=-1, keepdims=True)
    o = (o - mu) * lax.rsqrt(var + EPS)
    ret = o.reshape(B, S, RET_HEADS * RET_DV) * jax.nn.silu(g)
    mix = jnp.concatenate([lru, ret], axis=-1).astype(h.dtype)
    return mix @ w_out


def odd_mixer(h, lower_bound, w_in, hg_norm_w, conv_w, a_log, dt_bias, gd_norm_w, w_out):
    B, S, _ = h.shape
    proj = (h @ w_in).astype(jnp.float32)
    hq, hf, hi, hg, dq, dk_, dv_, dz, da, db = _split(proj, ODD_SIZES)
    lb = lower_bound.astype(jnp.float32)
    log_f = jnp.logaddexp(jnp.log(lb), jnp.log1p(-lb) + jax.nn.log_sigmoid(hf))
    key = -jnp.expm1(log_f)
    o_hg = hgrn2_chunkwise(hq.reshape(B, S, HG_HEADS, HG_DK), key.reshape(B, S, HG_HEADS, HG_DK),
                           hi.reshape(B, S, HG_HEADS, HG_DV), log_f.reshape(B, S, HG_HEADS, HG_DK))
    o_hg = rmsnorm(o_hg, hg_norm_w) * jax.nn.silu(hg.reshape(B, S, HG_HEADS, HG_DV))
    qkv = jax.nn.silu(causal_depthwise_conv(jnp.concatenate([dq, dk_, dv_], axis=-1), conv_w))
    dq, dk_, dv_ = _split(qkv, (GD_HEADS * GD_DK, GD_HEADS * GD_DK, GD_HEADS * GD_DV))
    q = l2norm(dq.reshape(B, S, GD_HEADS, GD_DK)) * (GD_DK ** -0.5)
    k = l2norm(dk_.reshape(B, S, GD_HEADS, GD_DK))
    beta = jax.nn.sigmoid(db)
    g = -jnp.exp(a_log.astype(jnp.float32)) * jax.nn.softplus(da + dt_bias.astype(jnp.float32))
    o_gd = gated_delta_chunkwise(q, k, dv_.reshape(B, S, GD_HEADS, GD_DV), g, beta)
    o_gd = rmsnorm(o_gd, gd_norm_w) * jax.nn.silu(dz.reshape(B, S, GD_HEADS, GD_DV))
    mix = jnp.concatenate([o_hg.reshape(B, S, -1), o_gd.reshape(B, S, -1)], axis=-1).astype(h.dtype)
    return mix @ w_out


def squared_relu_mlp(h, w1, w2):
    return jnp.square(jax.nn.relu(h @ w1)) @ w2


def hgrn_lower_bounds(logits):
    s = jax.nn.softmax(logits.astype(jnp.float32), axis=0)
    cs = jnp.cumsum(s, axis=0)
    return cs - cs[0:1]


def setup_inputs(seed: int = 0) -> dict:
    key = jax.random.key(seed)
    ks = jax.random.split(key, 32)
    nrm = lambda k, shape, s: jax.random.normal(k, shape, dtype=jnp.float32) * s
    D = D_MODEL
    x = nrm(ks[0], (BATCH, SEQ, D), 1.0)
    c = nrm(ks[1], (BATCH, D), 1.0)
    offset = jax.random.randint(ks[2], (BATCH, 1), 0, 1024, dtype=jnp.int32)
    positions = (offset + jnp.arange(SEQ, dtype=jnp.int32)[None, :]).astype(jnp.int32)
    ada_w = nrm(ks[3], (DEPTH, D, 6 * D), 0.3 * D ** -0.5)
    ada_b = nrm(ks[4], (DEPTH, 6 * D), 0.01)
    norm_mix_w = 1.0 + nrm(ks[5], (DEPTH, D), 0.05)
    norm_mlp_w = 1.0 + nrm(ks[6], (DEPTH, D), 0.05)
    mlp_w1 = nrm(ks[7], (DEPTH, D, MLP_HIDDEN), D ** -0.5)
    mlp_w2 = nrm(ks[8], (DEPTH, MLP_HIDDEN, D), MLP_HIDDEN ** -0.5)
    final_norm_w = 1.0 + nrm(ks[9], (D,), 0.05)
    ev_w_in = nrm(ks[10], (N_EVEN, D, EVEN_IN), D ** -0.5)
    lru_conv_w = nrm(ks[11], (N_EVEN, CONV_WIDTH, LRU_WIDTH), CONV_WIDTH ** -0.5)
    lru_conv_b = nrm(ks[12], (N_EVEN, LRU_WIDTH), 0.01)
    lru_w_a = nrm(ks[13], (N_EVEN, LRU_BLOCKS, LRU_BLOCK, LRU_BLOCK), LRU_BLOCK ** -0.5)
    lru_b_a = nrm(ks[14], (N_EVEN, LRU_WIDTH), 0.01)
    lru_w_x = nrm(ks[15], (N_EVEN, LRU_BLOCKS, LRU_BLOCK, LRU_BLOCK), LRU_BLOCK ** -0.5)
    lru_b_x = nrm(ks[16], (N_EVEN, LRU_WIDTH), 0.01)
    p = jax.random.uniform(ks[17], (N_EVEN, LRU_WIDTH), minval=0.9, maxval=0.999) ** (1.0 / LRU_C)
    lru_lambda = jnp.log(p) - jnp.log1p(-p)
    ev_w_out = nrm(ks[18], (N_EVEN, EVEN_MIX, D), EVEN_MIX ** -0.5)
    hg_lb_logits = nrm(ks[19], (DEPTH, HG_HEADS * HG_DK), 0.1)
    od_w_in = nrm(ks[20], (N_ODD, D, ODD_IN), D ** -0.5)
    hg_norm_w = 1.0 + nrm(ks[21], (N_ODD, HG_DV), 0.05)
    gd_conv_w = nrm(ks[22], (N_ODD, CONV_WIDTH, GD_HEADS * (2 * GD_DK + GD_DV)), CONV_WIDTH ** -0.5)
    gd_a_log = jnp.log(jax.random.uniform(ks[23], (N_ODD, GD_HEADS), minval=1.0, maxval=16.0))
    dt = jnp.exp(jax.random.uniform(ks[24], (N_ODD, GD_HEADS), minval=float(np.log(1e-3)), maxval=float(np.log(1e-1))))
    gd_dt_bias = dt + jnp.log(-jnp.expm1(-dt))
    gd_norm_w = 1.0 + nrm(ks[25], (N_ODD, GD_DV), 0.05)
    od_w_out = nrm(ks[26], (N_ODD, ODD_MIX, D), ODD_MIX ** -0.5)
    return {'x': x, 'c': c, 'positions': positions, 'ada_w': ada_w, 'ada_b': ada_b,
            'norm_mix_w': norm_mix_w, 'norm_mlp_w': norm_mlp_w, 'mlp_w1': mlp_w1, 'mlp_w2': mlp_w2,
            'final_norm_w': final_norm_w, 'ev_w_in': ev_w_in, 'lru_conv_w': lru_conv_w,
            'lru_conv_b': lru_conv_b, 'lru_w_a': lru_w_a, 'lru_b_a': lru_b_a, 'lru_w_x': lru_w_x,
            'lru_b_x': lru_b_x, 'lru_lambda': lru_lambda, 'ev_w_out': ev_w_out,
            'hg_lb_logits': hg_lb_logits, 'od_w_in': od_w_in, 'hg_norm_w': hg_norm_w,
            'gd_conv_w': gd_conv_w, 'gd_a_log': gd_a_log, 'gd_dt_bias': gd_dt_bias,
            'gd_norm_w': gd_norm_w, 'od_w_out': od_w_out}


def reference(x, c, positions, ada_w, ada_b, norm_mix_w, norm_mlp_w, mlp_w1, mlp_w2, final_norm_w,
              ev_w_in, lru_conv_w, lru_conv_b, lru_w_a, lru_b_a, lru_w_x, lru_b_x, lru_lambda, ev_w_out,
              hg_lb_logits, od_w_in, hg_norm_w, gd_conv_w, gd_a_log, gd_dt_bias, gd_norm_w, od_w_out):
    lower_bounds = hgrn_lower_bounds(hg_lb_logits)
    c_act = jax.nn.silu(c)
    for layer in range(DEPTH):
        mod = c_act @ ada_w[layer] + ada_b[layer]
        sh_m, sc_m, gt_m, sh_f, sc_f, gt_f = jnp.split(mod, 6, axis=-1)
        h = modulate(rmsnorm(x, norm_mix_w[layer]), sh_m, sc_m)
        j = layer // 2
        if layer % 2 == 0:
            y = even_mixer(h, positions, ev_w_in[j], lru_conv_w[j], lru_conv_b[j], lru_w_a[j], lru_b_a[j],
                           lru_w_x[j], lru_b_x[j], lru_lambda[j], ev_w_out[j])
        else:
            y = odd_mixer(h, lower_bounds[layer], od_w_in[j], hg_norm_w[j], gd_conv_w[j], gd_a_log[j],
                          gd_dt_bias[j], gd_norm_w[j], od_w_out[j])
        x = x + gt_m[:, None, :] * y
        h = modulate(rmsnorm(x, norm_mlp_w[layer]), sh_f, sc_f)
        x = x + gt_f[:, None, :] * squared_relu_mlp(h, mlp_w1[layer], mlp_w2[layer])
    return rmsnorm(x, final_norm_w)
```

```python
import functools
import math

import jax
import jax.numpy as jnp
from jax import lax
from jax.experimental import pallas as pl
from jax.experimental.pallas import tpu as pltpu

F32 = jnp.float32
BF16 = jnp.bfloat16

EPS = 1e-6
CONV_WIDTH = 4
LRU_C = 8.0
ROPE_BASE = 10000.0
HEADS = 4
HEAD_DIM = 128
HG_CHUNK = 32
GD_CHUNK = 64
SUBLANES = 8
LANES = 128
VMEM_LIMIT_BYTES = 56 * 1024 * 1024


def _dot(a, b):
    return jnp.dot(a, b, preferred_element_type=F32)


def _dot_nt(a, b):
    return lax.dot_general(a, b, (((1,), (1,)), ((), ())), preferred_element_type=F32)


def _dot_tn(a, b):
    return lax.dot_general(a, b, (((0,), (0,)), ((), ())), preferred_element_type=F32)


def _bf(x):
    return x.astype(BF16)


def _dot_exact_rhs(sel_bf16, x):
    hi = _bf(x)
    r1 = x - hi.astype(F32)
    mid = _bf(r1)
    lo = _bf(r1 - mid.astype(F32))
    return _dot(sel_bf16, hi) + _dot(sel_bf16, mid) + _dot(sel_bf16, lo)


def _softplus(x):
    return jnp.maximum(x, 0.0) + jnp.log1p(jnp.exp(-jnp.abs(x)))


def _neg_expm1(x):
    t = jnp.tanh(0.5 * x)
    return -2.0 * t / (1.0 - t)


def _silu(x):
    return x * jax.nn.sigmoid(x)


def _rmsnorm(x, w):
    return x * lax.rsqrt(jnp.mean(x * x, axis=-1, keepdims=True) + EPS) * w


def _shift_rows(x, prev8, s):
    rolled = pltpu.roll(x, s, axis=0)
    row = lax.broadcasted_iota(jnp.int32, (SUBLANES, x.shape[1]), 0)
    top = jnp.where(row < s, pltpu.roll(prev8, s, axis=0), rolled[0:SUBLANES])
    return jnp.concatenate([top, rolled[SUBLANES:]], axis=0)


def _causal_conv(x, prev8, w_ref, bias):
    acc = x * w_ref[CONV_WIDTH - 1:CONV_WIDTH, :]
    if bias is not None:
        acc = acc + bias
    for s in range(1, CONV_WIDTH):
        acc = acc + _shift_rows(x, prev8, s) * w_ref[CONV_WIDTH - 1 - s:CONV_WIDTH - s, :]
    return acc


def _block_masks(n, blk):
    r = lax.broadcasted_iota(jnp.int32, (n, n), 0)
    c = lax.broadcasted_iota(jnp.int32, (n, n), 1)
    sh = int(math.log2(blk))
    same = lax.shift_right_logical(r, sh) == lax.shift_right_logical(c, sh)
    return same & (c <= r), same & (c < r)


def _mod_kernel(c_ref, w_ref, b_ref, o_ref):
    c = c_ref[...]
    o_ref[0] = _dot(_bf(_silu(c)), _bf(w_ref[0])) + b_ref[0]


def _adaln_mod(c, ada_w, ada_b, tn=1536):
    depth, d, n = ada_w.shape
    b = c.shape[0]
    return pl.pallas_call(
        _mod_kernel,
        grid=(depth, n // tn),
        in_specs=[pl.BlockSpec((b, d), lambda l, j: (0, 0)),
                  pl.BlockSpec((1, d, tn), lambda l, j: (l, 0, j)),
                  pl.BlockSpec((1, 1, tn), lambda l, j: (l, 0, j))],
        out_specs=pl.BlockSpec((1, b, tn), lambda l, j: (l, 0, j)),
        out_shape=jax.ShapeDtypeStruct((depth, b, n), F32),
        compiler_params=pltpu.CompilerParams(
            dimension_semantics=("arbitrary", "arbitrary"), vmem_limit_bytes=VMEM_LIMIT_BYTES),
        name="adaln_mod",
    )(c, ada_w, ada_b.reshape(depth, 1, n))


def _lru_scan(a_s, u_s, h0, ts):
    width = a_s.shape[1]
    row = lax.broadcasted_iota(jnp.int32, (SUBLANES, width), 0)

    def body(g, hprev):
        off = pl.multiple_of(g * SUBLANES, SUBLANES)
        a8 = a_s[pl.ds(off, SUBLANES), :]
        u8 = u_s[pl.ds(off, SUBLANES), :]
        for d in (1, 2, 4):
            keep = row >= d
            a_sh = jnp.where(keep, pltpu.roll(a8, d, axis=0), 1.0)
            u_sh = jnp.where(keep, pltpu.roll(u8, d, axis=0), 0.0)
            u8 = u8 + a8 * u_sh
            a8 = a8 * a_sh
        h8 = u8 + a8 * hprev
        u_s[pl.ds(off, SUBLANES), :] = h8
        return h8[SUBLANES - 1:SUBLANES, :]

    return lax.fori_loop(0, ts // SUBLANES, body, h0)


def _even_kernel(x_ref, mod_ref, pos_ref, inv_ref, sgn_ref, nw_ref, win_ref, cw_ref, cb_ref,
                 wga_ref, wgx_ref, ba_ref, bx_ref, lam_ref, wout_ref, o_ref,
                 proj_s, a_s, u_s, mix_s, xprev_s, hprev_s, state_s, *, ts, chunk):
    lw = a_s.shape[1]
    half = lw // 2
    hd = HEAD_DIM

    @pl.when(pl.program_id(1) == 0)
    def _():
        xprev_s[...] = jnp.zeros_like(xprev_s)
        hprev_s[...] = jnp.zeros_like(hprev_s)
        state_s[...] = jnp.zeros_like(state_s)

    x = x_ref[0]
    shift, scale, gate = mod_ref[0, 0:1, :], mod_ref[0, 1:2, :], mod_ref[0, 2:3, :]
    h = _rmsnorm(x, nw_ref[...]) * (1.0 + scale) + shift
    proj_s[...] = _dot(_bf(h), win_ref[...])

    xr = proj_s[:, 0:lw]
    xc = _causal_conv(xr, xprev_s[...], cw_ref, cb_ref[...])
    xprev_s[...] = xr[ts - SUBLANES:ts, :]
    xcb = _bf(xc)
    r_pre = jnp.concatenate([_dot(xcb[:, :half], wga_ref[0]), _dot(xcb[:, half:], wga_ref[1])], axis=1)
    i_pre = jnp.concatenate([_dot(xcb[:, :half], wgx_ref[0]), _dot(xcb[:, half:], wgx_ref[1])], axis=1)
    r = jax.nn.sigmoid(r_pre + ba_ref[...])
    i = jax.nn.sigmoid(i_pre + bx_ref[...])
    log_a = (-LRU_C) * r * _softplus(-lam_ref[...])
    a_s[...] = jnp.exp(log_a)
    u_s[...] = jnp.sqrt(_neg_expm1(2.0 * log_a)) * (i * xc)
    hprev_s[...] = _lru_scan(a_s, u_s, hprev_s[...], ts)
    yr = proj_s[:, lw:2 * lw]
    mix_s[:, 0:lw] = _bf(u_s[...] * jax.nn.gelu(yr, approximate=True))

    ang = pos_ref[0] * inv_ref[...]
    cos_d = jnp.cos(ang)
    sin_d = jnp.sin(ang) * sgn_ref[...]
    ridx = lax.broadcasted_iota(jnp.int32, (chunk, chunk), 0)
    cidx = lax.broadcasted_iota(jnp.int32, (chunk, chunk), 1)
    rel = (ridx - cidx).astype(F32)
    tcol = lax.broadcasted_iota(jnp.int32, (chunk, 1), 0).astype(F32)
    qoff, koff, voff, goff = 2 * lw, 2 * lw + HEADS * hd, 2 * lw + 2 * HEADS * hd, 2 * lw + 3 * HEADS * hd
    for hh in range(HEADS):
        lg = math.log1p(-(2.0 ** (-5.0 - hh)))
        dmat = jnp.where(rel >= 0, jnp.exp(lg * jnp.maximum(rel, 0.0)), 0.0)
        q_dec = jnp.exp(lg * (tcol + 1.0))
        k_dec = jnp.exp(lg * (chunk - 1.0 - tcol))
        qh = proj_s[:, qoff + hh * hd:qoff + (hh + 1) * hd]
        kh = proj_s[:, koff + hh * hd:koff + (hh + 1) * hd]
        qh = qh * cos_d + pltpu.roll(qh, hd // 2, axis=1) * sin_d
        kh = (kh * cos_d + pltpu.roll(kh, hd // 2, axis=1) * sin_d) * (hd ** -0.5)
        for cc in range(ts // chunk):
            rows = slice(cc * chunk, (cc + 1) * chunk)
            qc, kc = qh[rows], kh[rows]
            vc = _bf(proj_s[rows, voff + hh * hd:voff + (hh + 1) * hd])
            qcb = _bf(qc)
            st = state_s[hh]
            scores = _dot_nt(qcb, _bf(kc)) * dmat
            o = _dot(_bf(scores), vc) + _dot(qcb, _bf(st)) * q_dec
            state_s[hh] = math.exp(lg * chunk) * st + _dot_tn(_bf(kc * k_dec), vc)
            mu = jnp.mean(o, axis=-1, keepdims=True)
            oc = o - mu
            var = jnp.mean(oc * oc, axis=-1, keepdims=True)
            g = proj_s[rows, goff + hh * hd:goff + (hh + 1) * hd]
            mix_s[rows, lw + hh * hd:lw + (hh + 1) * hd] = _bf(oc * lax.rsqrt(var + EPS) * _silu(g))

    o_ref[0] = x + gate * _dot(mix_s[...], wout_ref[...])


def _blockdiag_halves(w):
    n, bi, bj = w.shape
    eye = jnp.eye(n // 2, dtype=w.dtype)
    halves = []
    for p in range(2):
        blocks = w[p * (n // 2):(p + 1) * (n // 2)]
        dense = jnp.einsum('nm,nij->nimj', eye, blocks).reshape(n // 2 * bi, n // 2 * bj)
        halves.append(dense)
    return jnp.stack(halves)


def _const_spec(shape):
    nd = len(shape)
    return pl.BlockSpec(shape, lambda b, j: (0,) * nd)


def _even_mixer(x, mod, posf, norm_w, w_in, conv_w, conv_b, w_a, b_a, w_x, b_x, lam, w_out, ts=256, chunk=128):
    bsz, seq, d = x.shape
    lw = lam.shape[-1]
    n_in = w_in.shape[1]
    half = HEAD_DIM // 2
    inv = ROPE_BASE ** (-jnp.arange(half, dtype=F32) / half)
    inv2 = jnp.concatenate([inv, inv]).reshape(1, HEAD_DIM)
    sgn = jnp.concatenate([-jnp.ones((half,), F32), jnp.ones((half,), F32)]).reshape(1, HEAD_DIM)
    row = lambda v: v.reshape(1, -1).astype(F32)
    kern = functools.partial(_even_kernel, ts=ts, chunk=chunk)
    return pl.pallas_call(
        kern,
        grid=(bsz, seq // ts),
        in_specs=[pl.BlockSpec((1, ts, d), lambda b, j: (b, j, 0)),
                  pl.BlockSpec((1, 6, d), lambda b, j: (b, 0, 0)),
                  pl.BlockSpec((1, ts, 1), lambda b, j: (b, j, 0)),
                  _const_spec((1, HEAD_DIM)), _const_spec((1, HEAD_DIM)), _const_spec((1, d)),
                  _const_spec((d, n_in)), _const_spec((CONV_WIDTH, lw)), _const_spec((1, lw)),
                  _const_spec((2, lw // 2, lw // 2)), _const_spec((2, lw // 2, lw // 2)),
                  _const_spec((1, lw)), _const_spec((1, lw)), _const_spec((1, lw)),
                  _const_spec((lw + HEADS * HEAD_DIM, d))],
        out_specs=pl.BlockSpec((1, ts, d), lambda b, j: (b, j, 0)),
        out_shape=jax.ShapeDtypeStruct(x.shape, F32),
        scratch_shapes=[pltpu.VMEM((ts, n_in), F32), pltpu.VMEM((ts, lw), F32), pltpu.VMEM((ts, lw), F32),
                        pltpu.VMEM((ts, lw + HEADS * HEAD_DIM), BF16), pltpu.VMEM((SUBLANES, lw), F32),
                        pltpu.VMEM((1, lw), F32), pltpu.VMEM((HEADS, HEAD_DIM, HEAD_DIM), F32)],
        compiler_params=pltpu.CompilerParams(
            dimension_semantics=("arbitrary", "arbitrary"), vmem_limit_bytes=VMEM_LIMIT_BYTES),
        name="even_mixer",
    )(x, mod, posf, inv2, sgn, row(norm_w), _bf(w_in), conv_w.astype(F32), row(conv_b),
      _bf(_blockdiag_halves(w_a)), _bf(_blockdiag_halves(w_x)), row(b_a), row(b_x), row(lam), _bf(w_out))


def _odd_kernel(x_ref, mod_ref, nw_ref, win_ref, lbl_ref, hgw_ref, cw_ref, alog_ref, dtb_ref, gdw_ref,
                wout_ref, o_ref, proj_s, b_s, qi_s, ki_s, mix_s, cprev_s, hg_state_s, gd_state_s,
                *, ts, layer):
    hd = HEAD_DIM
    hw = HEADS * hd

    @pl.when(pl.program_id(1) == 0)
    def _():
        cprev_s[...] = jnp.zeros_like(cprev_s)
        hg_state_s[...] = jnp.zeros_like(hg_state_s)
        gd_state_s[...] = jnp.zeros_like(gd_state_s)

    x = x_ref[0]
    shift, scale, gate = mod_ref[0, 0:1, :], mod_ref[0, 1:2, :], mod_ref[0, 2:3, :]
    h = _rmsnorm(x, nw_ref[...]) * (1.0 + scale) + shift
    proj_s[...] = _dot(_bf(h), win_ref[...])

    logits = lbl_ref[...]
    e = jnp.exp(logits - jnp.max(logits, axis=0, keepdims=True))
    lb = jnp.sum(e[1:layer + 1], axis=0, keepdims=True) / jnp.sum(e, axis=0, keepdims=True)
    log_lb = jnp.log(lb)
    hf = proj_s[:, hw:2 * hw]
    other = jnp.log1p(-lb) - _softplus(-hf)
    log_f = jnp.maximum(log_lb, other) + jnp.log1p(jnp.exp(-jnp.abs(log_lb - other)))
    key = _neg_expm1(log_f)
    incl32, _ = _block_masks(ts, HG_CHUNK)
    b_s[...] = _dot_exact_rhs(jnp.where(incl32, 1.0, 0.0).astype(BF16), log_f)
    mid = HG_CHUNK // 2
    n_hg = ts // HG_CHUNK
    for cc in range(n_hg):
        rows = slice(cc * HG_CHUNK, (cc + 1) * HG_CHUNK)
        b = b_s[rows, :]
        b_mid = b[mid:mid + 1, :]
        qi_s[rows, :] = _bf(proj_s[rows, 0:hw] * jnp.exp(b - b_mid))
        ki_s[rows, :] = _bf(key[rows] * jnp.exp(b_mid - b))
    for hh in range(HEADS):
        cols = slice(hh * hd, (hh + 1) * hd)
        vh = _bf(proj_s[:, 2 * hw + hh * hd:2 * hw + (hh + 1) * hd])
        scores = jnp.where(incl32, _dot_nt(qi_s[:, cols], ki_s[:, cols]), 0.0)
        o_intra = _dot(_bf(scores), vh)
        gh = proj_s[:, 3 * hw + hh * hd:3 * hw + (hh + 1) * hd]
        for cc in range(n_hg):
            rows = slice(cc * HG_CHUNK, (cc + 1) * HG_CHUNK)
            b = b_s[rows, cols]
            b_last = b[HG_CHUNK - 1:HG_CHUNK, :]
            st_t = hg_state_s[hh]
            qs = _bf(proj_s[rows, cols] * jnp.exp(b))
            o = o_intra[rows] + _dot_nt(qs, _bf(st_t))
            kd = _bf(key[rows, cols] * jnp.exp(b_last - b))
            hg_state_s[hh] = st_t * jnp.exp(b_last) + _dot_tn(vh[rows], kd)
            on = o * lax.rsqrt(jnp.mean(o * o, axis=-1, keepdims=True) + EPS) * hgw_ref[...]
            mix_s[rows, cols] = _bf(on * _silu(gh[rows]))

    qkv_raw = proj_s[:, 4 * hw:7 * hw]
    qkv = _silu(_causal_conv(qkv_raw, cprev_s[...], cw_ref, None))
    cprev_s[...] = qkv_raw[ts - SUBLANES:ts, :]
    ab = proj_s[:, 8 * hw:8 * hw + LANES]
    g_all = -jnp.exp(alog_ref[...]) * _softplus(ab + dtb_ref[...])
    beta_all = jax.nn.sigmoid(ab)
    incl64, strict64 = _block_masks(ts, GD_CHUNK)
    gcum = _dot_exact_rhs(jnp.where(incl64, 1.0, 0.0).astype(BF16), g_all)
    gcum_t = gcum.T
    r = lax.broadcasted_iota(jnp.int32, (ts, ts), 0)
    c = lax.broadcasted_iota(jnp.int32, (ts, ts), 1)
    eye = jnp.where(r == c, 1.0, 0.0)
    n_gd = ts // GD_CHUNK
    for hh in range(HEADS):
        cols = slice(hh * hd, (hh + 1) * hd)
        qh = qkv[:, hh * hd:(hh + 1) * hd]
        kh = qkv[:, hw + hh * hd:hw + (hh + 1) * hd]
        vh = qkv[:, 2 * hw + hh * hd:2 * hw + (hh + 1) * hd]
        qh = qh * lax.rsqrt(jnp.sum(qh * qh, axis=-1, keepdims=True) + EPS) * (hd ** -0.5)
        kh = kh * lax.rsqrt(jnp.sum(kh * kh, axis=-1, keepdims=True) + EPS)
        gcol = gcum[:, hh:hh + 1]
        grow = gcum_t[hh:hh + 1, :]
        beta = beta_all[:, HEADS + hh:HEADS + hh + 1]
        decay = jnp.where(incl64, jnp.exp(jnp.where(incl64, gcol - grow, 0.0)), 0.0)
        kb = kh * beta
        khb = _bf(kh)
        a_mat = jnp.where(strict64, _dot_nt(_bf(kb), khb) * decay, 0.0)
        qk = jnp.where(incl64, _dot_nt(_bf(qh), khb) * decay, 0.0)
        t_inv = eye - a_mat
        pw = a_mat
        for _ in range(int(math.log2(GD_CHUNK)) - 1):
            pwb = _bf(pw)
            pw = _dot(pwb, pwb)
            t_inv = t_inv + _dot(_bf(t_inv), _bf(pw))
        egc = jnp.exp(gcol)
        rhs = jnp.concatenate([vh * beta, kb * egc], axis=1)
        sol = _dot(_bf(t_inv), _bf(rhs))
        u_all, w_all = sol[:, :hd], sol[:, hd:]
        q_g = _bf(qh * egc)
        zh = proj_s[:, 7 * hw + hh * hd:7 * hw + (hh + 1) * hd]
        for cc in range(n_gd):
            rows = slice(cc * GD_CHUNK, (cc + 1) * GD_CHUNK)
            g_last = gcol[(cc + 1) * GD_CHUNK - 1:(cc + 1) * GD_CHUNK, :]
            st = gd_state_s[hh]
            stb = _bf(st)
            v_new = u_all[rows] - _dot(_bf(w_all[rows]), stb)
            v_newb = _bf(v_new)
            o = _dot(q_g[rows], stb) + _dot(_bf(qk[rows, rows]), v_newb)
            k_g = _bf(kh[rows] * jnp.exp(g_last - gcol[rows]))
            gd_state_s[hh] = st * jnp.exp(g_last) + _dot_tn(k_g, v_newb)
            on = o * lax.rsqrt(jnp.mean(o * o, axis=-1, keepdims=True) + EPS) * gdw_ref[...]
            mix_s[rows, hw + hh * hd:hw + (hh + 1) * hd] = _bf(on * _silu(zh[rows]))

    o_ref[0] = x + gate * _dot(mix_s[...], wout_ref[...])


def _odd_mixer(x, mod, norm_w, lb_logits, layer, w_in, hg_norm_w, conv_w, a_log, dt_bias, gd_norm_w, w_out, ts=256):
    bsz, seq, d = x.shape
    hw = HEADS * HEAD_DIM
    n_in = 8 * hw + LANES
    pad = n_in - w_in.shape[1]
    w_in_p = jnp.concatenate([w_in, jnp.zeros((d, pad), w_in.dtype)], axis=1)
    lane_pad = lambda v, off: jnp.zeros((1, LANES), F32).at[0, off:off + v.shape[0]].set(v.astype(F32))
    row = lambda v: v.reshape(1, -1).astype(F32)
    depth = lb_logits.shape[0]
    kern = functools.partial(_odd_kernel, ts=ts, layer=layer)
    return pl.pallas_call(
        kern,
        grid=(bsz, seq // ts),
        in_specs=[pl.BlockSpec((1, ts, d), lambda b, j: (b, j, 0)),
                  pl.BlockSpec((1, 6, d), lambda b, j: (b, 0, 0)),
                  _const_spec((1, d)), _const_spec((d, n_in)), _const_spec((depth, hw)),
                  _const_spec((1, HEAD_DIM)), _const_spec((CONV_WIDTH, 3 * hw)),
                  _const_spec((1, LANES)), _const_spec((1, LANES)), _const_spec((1, HEAD_DIM)),
                  _const_spec((2 * hw, d))],
        out_specs=pl.BlockSpec((1, ts, d), lambda b, j: (b, j, 0)),
        out_shape=jax.ShapeDtypeStruct(x.shape, F32),
        scratch_shapes=[pltpu.VMEM((ts, n_in), F32), pltpu.VMEM((ts, hw), F32),
                        pltpu.VMEM((ts, hw), BF16), pltpu.VMEM((ts, hw), BF16),
                        pltpu.VMEM((ts, 2 * hw), BF16), pltpu.VMEM((SUBLANES, 3 * hw), F32),
                        pltpu.VMEM((HEADS, HEAD_DIM, HEAD_DIM), F32), pltpu.VMEM((HEADS, HEAD_DIM, HEAD_DIM), F32)],
        compiler_params=pltpu.CompilerParams(
            dimension_semantics=("arbitrary", "arbitrary"), vmem_limit_bytes=VMEM_LIMIT_BYTES),
        name="odd_mixer",
    )(x, mod, row(norm_w), _bf(w_in_p), lb_logits.astype(F32), row(hg_norm_w), conv_w.astype(F32),
      lane_pad(a_log, 0), lane_pad(dt_bias, 0), row(gd_norm_w), _bf(w_out))


def _mlp_kernel(x_ref, mod_ref, nw_ref, w1_ref, w2_ref, fw_ref, o_ref, *, n_split, final):
    x = x_ref[0]
    shift, scale, gate = mod_ref[0, 3:4, :], mod_ref[0, 4:5, :], mod_ref[0, 5:6, :]
    h = _bf(_rmsnorm(x, nw_ref[...]) * (1.0 + scale) + shift)
    hidden = w1_ref.shape[1]
    step = hidden // n_split
    acc = jnp.zeros(x.shape, F32)
    for s in range(n_split):
        hid = jnp.square(jnp.maximum(_dot(h, w1_ref[:, s * step:(s + 1) * step]), 0.0))
        acc = acc + _dot(_bf(hid), w2_ref[s * step:(s + 1) * step, :])
    y = x + gate * acc
    if final:
        y = _rmsnorm(y, fw_ref[...])
    o_ref[0] = y


def _mlp(x, mod, norm_w, w1, w2, final_w, final, tm=512, n_split=4):
    bsz, seq, d = x.shape
    hidden = w1.shape[1]
    kern = functools.partial(_mlp_kernel, n_split=n_split, final=final)
    row = lambda v: v.reshape(1, -1).astype(F32)
    return pl.pallas_call(
        kern,
        grid=(bsz, seq // tm),
        in_specs=[pl.BlockSpec((1, tm, d), lambda b, j: (b, j, 0)),
                  pl.BlockSpec((1, 6, d), lambda b, j: (b, 0, 0)),
                  _const_spec((1, d)), _const_spec((d, hidden)), _const_spec((hidden, d)), _const_spec((1, d))],
        out_specs=pl.BlockSpec((1, tm, d), lambda b, j: (b, j, 0)),
        out_shape=jax.ShapeDtypeStruct(x.shape, F32),
        compiler_params=pltpu.CompilerParams(
            dimension_semantics=("arbitrary", "arbitrary"), vmem_limit_bytes=VMEM_LIMIT_BYTES),
        name="mlp_final" if final else "mlp",
    )(x, mod, row(norm_w), _bf(w1), _bf(w2), row(final_w))


def kernel(x, c, positions, ada_w, ada_b, norm_mix_w, norm_mlp_w, mlp_w1, mlp_w2, final_norm_w, ev_w_in, lru_conv_w, lru_conv_b, lru_w_a, lru_b_a, lru_w_x, lru_b_x, lru_lambda, ev_w_out, hg_lb_logits, od_w_in, hg_norm_w, gd_conv_w, gd_a_log, gd_dt_bias, gd_norm_w, od_w_out):
    depth = ada_w.shape[0]
    bsz, seq, d = x.shape
    mod = _adaln_mod(c, ada_w, ada_b).reshape(depth, bsz, 6, d)
    posf = positions.astype(F32)[..., None]
    for layer in range(depth):
        j = layer // 2
        if layer % 2 == 0:
            x = _even_mixer(x, mod[layer], posf, norm_mix_w[layer], ev_w_in[j], lru_conv_w[j], lru_conv_b[j],
                            lru_w_a[j], lru_b_a[j], lru_w_x[j], lru_b_x[j], lru_lambda[j], ev_w_out[j])
        else:
            x = _odd_mixer(x, mod[layer], norm_mix_w[layer], hg_lb_logits, layer, od_w_in[j], hg_norm_w[j],
                           gd_conv_w[j], gd_a_log[j], gd_dt_bias[j], gd_norm_w[j], od_w_out[j])
        x = _mlp(x, mod[layer], norm_mlp_w[layer], mlp_w1[layer], mlp_w2[layer], final_norm_w,
                 final=(layer == depth - 1))
    return x
```

```python
import functools
import math

import jax
import jax.numpy as jnp
from jax import lax
from jax.experimental import pallas as pl
from jax.experimental.pallas import tpu as pltpu

F32 = jnp.float32
BF16 = jnp.bfloat16

EPS = 1e-6
CONV_WIDTH = 4
LRU_C = 8.0
ROPE_BASE = 10000.0
HEADS = 4
HEAD_DIM = 128
HG_CHUNK = 32
GD_CHUNK = 64
SUBLANES = 8
LANES = 128
VMEM_LIMIT_BYTES = 56 * 1024 * 1024


def _dot(a, b):
    return jnp.dot(a, b, preferred_element_type=F32)


def _dot_nt(a, b):
    return lax.dot_general(a, b, (((1,), (1,)), ((), ())), preferred_element_type=F32)


def _dot_tn(a, b):
    return lax.dot_general(a, b, (((0,), (0,)), ((), ())), preferred_element_type=F32)


def _bf(x):
    return x.astype(BF16)


def _dot_exact_rhs(sel_bf16, x):
    hi = _bf(x)
    r1 = x - hi.astype(F32)
    mid = _bf(r1)
    lo = _bf(r1 - mid.astype(F32))
    return _dot(sel_bf16, hi) + _dot(sel_bf16, mid) + _dot(sel_bf16, lo)


def _softplus(x):
    return jnp.maximum(x, 0.0) + jnp.log1p(jnp.exp(-jnp.abs(x)))


def _neg_expm1(x):
    t = jnp.tanh(0.5 * x)
    return -2.0 * t / (1.0 - t)


def _silu(x):
    return x * jax.nn.sigmoid(x)


def _rmsnorm(x, w):
    return x * lax.rsqrt(jnp.mean(x * x, axis=-1, keepdims=True) + EPS) * w


def _shift_rows(x, prev8, s):
    rolled = pltpu.roll(x, s, axis=0)
    row = lax.broadcasted_iota(jnp.int32, (SUBLANES, x.shape[1]), 0)
    top = jnp.where(row < s, pltpu.roll(prev8, s, axis=0), rolled[0:SUBLANES])
    return jnp.concatenate([top, rolled[SUBLANES:]], axis=0)


def _causal_conv(x, prev8, w_ref, bias):
    acc = x * w_ref[CONV_WIDTH - 1:CONV_WIDTH, :]
    if bias is not None:
        acc = acc + bias
    for s in range(1, CONV_WIDTH):
        acc = acc + _shift_rows(x, prev8, s) * w_ref[CONV_WIDTH - 1 - s:CONV_WIDTH - s, :]
    return acc


def _block_masks(n, blk):
    r = lax.broadcasted_iota(jnp.int32, (n, n), 0)
    c = lax.broadcasted_iota(jnp.int32, (n, n), 1)
    sh = int(math.log2(blk))
    same = lax.shift_right_logical(r, sh) == lax.shift_right_logical(c, sh)
    return same & (c <= r), same & (c < r)


def _mod_kernel(c_ref, w_ref, b_ref, o_ref):
    c = c_ref[...]
    o_ref[0] = _dot(_bf(_silu(c)), _bf(w_ref[0])) + b_ref[0]


def _adaln_mod(c, ada_w, ada_b, tn=1536):
    depth, d, n = ada_w.shape
    b = c.shape[0]
    return pl.pallas_call(
        _mod_kernel,
        grid=(depth, n // tn),
        in_specs=[pl.BlockSpec((b, d), lambda l, j: (0, 0)),
                  pl.BlockSpec((1, d, tn), lambda l, j: (l, 0, j)),
                  pl.BlockSpec((1, 1, tn), lambda l, j: (l, 0, j))],
        out_specs=pl.BlockSpec((1, b, tn), lambda l, j: (l, 0, j)),
        out_shape=jax.ShapeDtypeStruct((depth, b, n), F32),
        compiler_params=pltpu.CompilerParams(
            dimension_semantics=("arbitrary", "arbitrary"), vmem_limit_bytes=VMEM_LIMIT_BYTES),
        name="adaln_mod",
    )(c, ada_w, ada_b.reshape(depth, 1, n))


def _lru_scan(a_s, u_s, h0, ts):
    width = a_s.shape[1]
    row = lax.broadcasted_iota(jnp.int32, (SUBLANES, width), 0)

    def body(g, hprev):
        off = pl.multiple_of(g * SUBLANES, SUBLANES)
        a8 = a_s[pl.ds(off, SUBLANES), :]
        u8 = u_s[pl.ds(off, SUBLANES), :]
        for d in (1, 2, 4):
            keep = row >= d
            a_sh = jnp.where(keep, pltpu.roll(a8, d, axis=0), 1.0)
            u_sh = jnp.where(keep, pltpu.roll(u8, d, axis=0), 0.0)
            u8 = u8 + a8 * u_sh
            a8 = a8 * a_sh
        h8 = u8 + a8 * hprev
        u_s[pl.ds(off, SUBLANES), :] = h8
        return h8[SUBLANES - 1:SUBLANES, :]

    return lax.fori_loop(0, ts // SUBLANES, body, h0)


def _even_kernel(x_ref, mod_ref, pos_ref, inv_ref, sgn_ref, nw_ref, win_ref, cw_ref, cb_ref,
                 wga_ref, wgx_ref, ba_ref, bx_ref, lam_ref, wout_ref, o_ref,
                 proj_s, a_s, u_s, mix_s, xprev_s, hprev_s, state_s, *, ts, chunk):
    lw = a_s.shape[1]
    half = lw // 2
    hd = HEAD_DIM

    @pl.when(pl.program_id(1) == 0)
    def _():
        xprev_s[...] = jnp.zeros_like(xprev_s)
        hprev_s[...] = jnp.zeros_like(hprev_s)
        state_s[...] = jnp.zeros_like(state_s)

    x = x_ref[0]
    shift, scale, gate = mod_ref[0, 0:1, :], mod_ref[0, 1:2, :], mod_ref[0, 2:3, :]
    h = _rmsnorm(x, nw_ref[...]) * (1.0 + scale) + shift
    proj_s[...] = _dot(_bf(h), win_ref[...])

    xr = proj_s[:, 0:lw]
    xc = _causal_conv(xr, xprev_s[...], cw_ref, cb_ref[...])
    xprev_s[...] = xr[ts - SUBLANES:ts, :]
    xcb = _bf(xc)
    r_pre = jnp.concatenate([_dot(xcb[:, :half], wga_ref[0]), _dot(xcb[:, half:], wga_ref[1])], axis=1)
    i_pre = jnp.concatenate([_dot(xcb[:, :half], wgx_ref[0]), _dot(xcb[:, half:], wgx_ref[1])], axis=1)
    r = jax.nn.sigmoid(r_pre + ba_ref[...])
    i = jax.nn.sigmoid(i_pre + bx_ref[...])
    log_a = (-LRU_C) * r * _softplus(-lam_ref[...])
    a_s[...] = jnp.exp(log_a)
    u_s[...] = jnp.sqrt(_neg_expm1(2.0 * log_a)) * (i * xc)
    hprev_s[...] = _lru_scan(a_s, u_s, hprev_s[...], ts)
    yr = proj_s[:, lw:2 * lw]
    mix_s[:, 0:lw] = _bf(u_s[...] * jax.nn.gelu(yr, approximate=True))

    ang = pos_ref[0] * inv_ref[...]
    cos_d = jnp.cos(ang)
    sin_d = jnp.sin(ang) * sgn_ref[...]
    ridx = lax.broadcasted_iota(jnp.int32, (chunk, chunk), 0)
    cidx = lax.broadcasted_iota(jnp.int32, (chunk, chunk), 1)
    rel = (ridx - cidx).astype(F32)
    tcol = lax.broadcasted_iota(jnp.int32, (chunk, 1), 0).astype(F32)
    qoff, koff, voff, goff = 2 * lw, 2 * lw + HEADS * hd, 2 * lw + 2 * HEADS * hd, 2 * lw + 3 * HEADS * hd
    log_gamma = [math.log1p(-(2.0 ** (-5.0 - hh))) for hh in range(HEADS)]
    dmats = [jnp.where(rel >= 0, jnp.exp(lg * jnp.maximum(rel, 0.0)), 0.0) for lg in log_gamma]
    q_decs = [jnp.exp(lg * (tcol + 1.0)) for lg in log_gamma]
    k_decs = [jnp.exp(lg * (chunk - 1.0 - tcol)) for lg in log_gamma]
    for cc in range(ts // chunk):
        rows = slice(cc * chunk, (cc + 1) * chunk)
        cos_c, sin_c = cos_d[rows], sin_d[rows]
        for hh, lg in enumerate(log_gamma):
            qc = proj_s[rows, qoff + hh * hd:qoff + (hh + 1) * hd]
            kc = proj_s[rows, koff + hh * hd:koff + (hh + 1) * hd]
            qc = qc * cos_c + pltpu.roll(qc, hd // 2, axis=1) * sin_c
            kc = (kc * cos_c + pltpu.roll(kc, hd // 2, axis=1) * sin_c) * (hd ** -0.5)
            vc = _bf(proj_s[rows, voff + hh * hd:voff + (hh + 1) * hd])
            qcb = _bf(qc)
            st = state_s[hh]
            scores = _dot_nt(qcb, _bf(kc)) * dmats[hh]
            o = _dot(_bf(scores), vc) + _dot(qcb, _bf(st)) * q_decs[hh]
            state_s[hh] = math.exp(lg * chunk) * st + _dot_tn(_bf(kc * k_decs[hh]), vc)
            mu = jnp.mean(o, axis=-1, keepdims=True)
            oc = o - mu
            var = jnp.mean(oc * oc, axis=-1, keepdims=True)
            g = proj_s[rows, goff + hh * hd:goff + (hh + 1) * hd]
            mix_s[rows, lw + hh * hd:lw + (hh + 1) * hd] = _bf(oc * lax.rsqrt(var + EPS) * _silu(g))

    o_ref[0] = x + gate * _dot(mix_s[...], wout_ref[...])


def _blockdiag_halves(w):
    n, bi, bj = w.shape
    eye = jnp.eye(n // 2, dtype=w.dtype)
    halves = []
    for p in range(2):
        blocks = w[p * (n // 2):(p + 1) * (n // 2)]
        dense = jnp.einsum('nm,nij->nimj', eye, blocks).reshape(n // 2 * bi, n // 2 * bj)
        halves.append(dense)
    return jnp.stack(halves)


def _const_spec(shape):
    nd = len(shape)
    return pl.BlockSpec(shape, lambda b, j: (0,) * nd)


def _even_mixer(x, mod, posf, norm_w, w_in, conv_w, conv_b, w_a, b_a, w_x, b_x, lam, w_out, ts=256, chunk=128):
    bsz, seq, d = x.shape
    lw = lam.shape[-1]
    n_in = w_in.shape[1]
    half = HEAD_DIM // 2
    inv = ROPE_BASE ** (-jnp.arange(half, dtype=F32) / half)
    inv2 = jnp.concatenate([inv, inv]).reshape(1, HEAD_DIM)
    sgn = jnp.concatenate([-jnp.ones((half,), F32), jnp.ones((half,), F32)]).reshape(1, HEAD_DIM)
    row = lambda v: v.reshape(1, -1).astype(F32)
    kern = functools.partial(_even_kernel, ts=ts, chunk=chunk)
    return pl.pallas_call(
        kern,
        grid=(bsz, seq // ts),
        in_specs=[pl.BlockSpec((1, ts, d), lambda b, j: (b, j, 0)),
                  pl.BlockSpec((1, 6, d), lambda b, j: (b, 0, 0)),
                  pl.BlockSpec((1, ts, 1), lambda b, j: (b, j, 0)),
                  _const_spec((1, HEAD_DIM)), _const_spec((1, HEAD_DIM)), _const_spec((1, d)),
                  _const_spec((d, n_in)), _const_spec((CONV_WIDTH, lw)), _const_spec((1, lw)),
                  _const_spec((2, lw // 2, lw // 2)), _const_spec((2, lw // 2, lw // 2)),
                  _const_spec((1, lw)), _const_spec((1, lw)), _const_spec((1, lw)),
                  _const_spec((lw + HEADS * HEAD_DIM, d))],
        out_specs=pl.BlockSpec((1, ts, d), lambda b, j: (b, j, 0)),
        out_shape=jax.ShapeDtypeStruct(x.shape, F32),
        scratch_shapes=[pltpu.VMEM((ts, n_in), F32), pltpu.VMEM((ts, lw), F32), pltpu.VMEM((ts, lw), F32),
                        pltpu.VMEM((ts, lw + HEADS * HEAD_DIM), BF16), pltpu.VMEM((SUBLANES, lw), F32),
                        pltpu.VMEM((1, lw), F32), pltpu.VMEM((HEADS, HEAD_DIM, HEAD_DIM), F32)],
        compiler_params=pltpu.CompilerParams(
            dimension_semantics=("arbitrary", "arbitrary"), vmem_limit_bytes=VMEM_LIMIT_BYTES),
        name="even_mixer",
    )(x, mod, posf, inv2, sgn, row(norm_w), _bf(w_in), conv_w.astype(F32), row(conv_b),
      _bf(_blockdiag_halves(w_a)), _bf(_blockdiag_halves(w_x)), row(b_a), row(b_x), row(lam), _bf(w_out))


def _odd_kernel(x_ref, mod_ref, nw_ref, win_ref, lbl_ref, hgw_ref, cw_ref, alog_ref, dtb_ref, gdw_ref,
                wout_ref, o_ref, proj_s, b_s, qi_s, ki_s, mix_s, cprev_s, hg_state_s, gd_state_s,
                *, ts, layer):
    hd = HEAD_DIM
    hw = HEADS * hd

    @pl.when(pl.program_id(1) == 0)
    def _():
        cprev_s[...] = jnp.zeros_like(cprev_s)
        hg_state_s[...] = jnp.zeros_like(hg_state_s)
        gd_state_s[...] = jnp.zeros_like(gd_state_s)

    x = x_ref[0]
    shift, scale, gate = mod_ref[0, 0:1, :], mod_ref[0, 1:2, :], mod_ref[0, 2:3, :]
    h = _rmsnorm(x, nw_ref[...]) * (1.0 + scale) + shift
    proj_s[...] = _dot(_bf(h), win_ref[...])

    logits = lbl_ref[...]
    e = jnp.exp(logits - jnp.max(logits, axis=0, keepdims=True))
    lb = jnp.sum(e[1:layer + 1], axis=0, keepdims=True) / jnp.sum(e, axis=0, keepdims=True)
    log_lb = jnp.log(lb)
    hf = proj_s[:, hw:2 * hw]
    other = jnp.log1p(-lb) - _softplus(-hf)
    log_f = jnp.maximum(log_lb, other) + jnp.log1p(jnp.exp(-jnp.abs(log_lb - other)))
    key = _neg_expm1(log_f)
    incl32, _ = _block_masks(ts, HG_CHUNK)
    b_s[...] = _dot_exact_rhs(jnp.where(incl32, 1.0, 0.0).astype(BF16), log_f)
    mid = HG_CHUNK // 2
    n_hg = ts // HG_CHUNK
    for cc in range(n_hg):
        rows = slice(cc * HG_CHUNK, (cc + 1) * HG_CHUNK)
        b = b_s[rows, :]
        b_mid = b[mid:mid + 1, :]
        qi_s[rows, :] = _bf(proj_s[rows, 0:hw] * jnp.exp(b - b_mid))
        ki_s[rows, :] = _bf(key[rows] * jnp.exp(b_mid - b))
    head_cols = [slice(hh * hd, (hh + 1) * hd) for hh in range(HEADS)]
    hg_v = [_bf(proj_s[:, 2 * hw + hh * hd:2 * hw + (hh + 1) * hd]) for hh in range(HEADS)]
    hg_scores = [jnp.where(incl32, _dot_nt(qi_s[:, cols], ki_s[:, cols]), 0.0) for cols in head_cols]
    hg_intra = [_dot(_bf(sc), v) for sc, v in zip(hg_scores, hg_v)]
    for cc in range(n_hg):
        rows = slice(cc * HG_CHUNK, (cc + 1) * HG_CHUNK)
        for hh, cols in enumerate(head_cols):
            b = b_s[rows, cols]
            b_last = b[HG_CHUNK - 1:HG_CHUNK, :]
            st_t = hg_state_s[hh]
            qs = _bf(proj_s[rows, cols] * jnp.exp(b))
            o = hg_intra[hh][rows] + _dot_nt(qs, _bf(st_t))
            kd = _bf(key[rows, cols] * jnp.exp(b_last - b))
            hg_state_s[hh] = st_t * jnp.exp(b_last) + _dot_tn(hg_v[hh][rows], kd)
            on = o * lax.rsqrt(jnp.mean(o * o, axis=-1, keepdims=True) + EPS) * hgw_ref[...]
            gh = proj_s[rows, 3 * hw + hh * hd:3 * hw + (hh + 1) * hd]
            mix_s[rows, cols] = _bf(on * _silu(gh))

    qkv_raw = proj_s[:, 4 * hw:7 * hw]
    qkv = _silu(_causal_conv(qkv_raw, cprev_s[...], cw_ref, None))
    cprev_s[...] = qkv_raw[ts - SUBLANES:ts, :]
    ab = proj_s[:, 8 * hw:8 * hw + LANES]
    g_all = -jnp.exp(alog_ref[...]) * _softplus(ab + dtb_ref[...])
    beta_all = jax.nn.sigmoid(ab)
    incl64, strict64 = _block_masks(ts, GD_CHUNK)
    gcum = _dot_exact_rhs(jnp.where(incl64, 1.0, 0.0).astype(BF16), g_all)
    gcum_t = gcum.T
    r = lax.broadcasted_iota(jnp.int32, (ts, ts), 0)
    c = lax.broadcasted_iota(jnp.int32, (ts, ts), 1)
    eye = jnp.where(r == c, 1.0, 0.0)
    n_gd = ts // GD_CHUNK
    gd_q, gd_k, gd_gcol, gd_qk, a_mats, rhs_all = [], [], [], [], [], []
    for hh in range(HEADS):
        qh = qkv[:, hh * hd:(hh + 1) * hd]
        kh = qkv[:, hw + hh * hd:hw + (hh + 1) * hd]
        vh = qkv[:, 2 * hw + hh * hd:2 * hw + (hh + 1) * hd]
        qh = qh * lax.rsqrt(jnp.sum(qh * qh, axis=-1, keepdims=True) + EPS) * (hd ** -0.5)
        kh = kh * lax.rsqrt(jnp.sum(kh * kh, axis=-1, keepdims=True) + EPS)
        gcol = gcum[:, hh:hh + 1]
        grow = gcum_t[hh:hh + 1, :]
        beta = beta_all[:, HEADS + hh:HEADS + hh + 1]
        decay = jnp.where(incl64, jnp.exp(jnp.where(incl64, gcol - grow, 0.0)), 0.0)
        kb = kh * beta
        khb = _bf(kh)
        a_mats.append(jnp.where(strict64, _dot_nt(_bf(kb), khb) * decay, 0.0))
        gd_qk.append(_bf(jnp.where(incl64, _dot_nt(_bf(qh), khb) * decay, 0.0)))
        egc = jnp.exp(gcol)
        rhs_all.append(_bf(jnp.concatenate([vh * beta, kb * egc], axis=1)))
        gd_q.append(_bf(qh * egc))
        gd_k.append(kh)
        gd_gcol.append(gcol)
    t_inv = [eye - a for a in a_mats]
    pw = a_mats
    for _ in range(int(math.log2(GD_CHUNK)) - 1):
        pwb = [_bf(p) for p in pw]
        pw = [_dot(p, p) for p in pwb]
        t_inv = [t + _dot(_bf(t), _bf(p)) for t, p in zip(t_inv, pw)]
    sol = [_dot(_bf(t), rh) for t, rh in zip(t_inv, rhs_all)]
    for cc in range(n_gd):
        rows = slice(cc * GD_CHUNK, (cc + 1) * GD_CHUNK)
        for hh in range(HEADS):
            gcol = gd_gcol[hh]
            g_last = gcol[(cc + 1) * GD_CHUNK - 1:(cc + 1) * GD_CHUNK, :]
            st = gd_state_s[hh]
            stb = _bf(st)
            v_new = sol[hh][rows, :hd] - _dot(_bf(sol[hh][rows, hd:]), stb)
            v_newb = _bf(v_new)
            o = _dot(gd_q[hh][rows], stb) + _dot(gd_qk[hh][rows, rows], v_newb)
            k_g = _bf(gd_k[hh][rows] * jnp.exp(g_last - gcol[rows]))
            gd_state_s[hh] = st * jnp.exp(g_last) + _dot_tn(k_g, v_newb)
            on = o * lax.rsqrt(jnp.mean(o * o, axis=-1, keepdims=True) + EPS) * gdw_ref[...]
            zh = proj_s[rows, 7 * hw + hh * hd:7 * hw + (hh + 1) * hd]
            mix_s[rows, hw + hh * hd:hw + (hh + 1) * hd] = _bf(on * _silu(zh))

    o_ref[0] = x + gate * _dot(mix_s[...], wout_ref[...])


def _odd_mixer(x, mod, norm_w, lb_logits, layer, w_in, hg_norm_w, conv_w, a_log, dt_bias, gd_norm_w, w_out, ts=256):
    bsz, seq, d = x.shape
    hw = HEADS * HEAD_DIM
    n_in = 8 * hw + LANES
    pad = n_in - w_in.shape[1]
    w_in_p = jnp.concatenate([w_in, jnp.zeros((d, pad), w_in.dtype)], axis=1)
    lane_pad = lambda v, off: jnp.zeros((1, LANES), F32).at[0, off:off + v.shape[0]].set(v.astype(F32))
    row = lambda v: v.reshape(1, -1).astype(F32)
    depth = lb_logits.shape[0]
    kern = functools.partial(_odd_kernel, ts=ts, layer=layer)
    return pl.pallas_call(
        kern,
        grid=(bsz, seq // ts),
        in_specs=[pl.BlockSpec((1, ts, d), lambda b, j: (b, j, 0)),
                  pl.BlockSpec((1, 6, d), lambda b, j: (b, 0, 0)),
                  _const_spec((1, d)), _const_spec((d, n_in)), _const_spec((depth, hw)),
                  _const_spec((1, HEAD_DIM)), _const_spec((CONV_WIDTH, 3 * hw)),
                  _const_spec((1, LANES)), _const_spec((1, LANES)), _const_spec((1, HEAD_DIM)),
                  _const_spec((2 * hw, d))],
        out_specs=pl.BlockSpec((1, ts, d), lambda b, j: (b, j, 0)),
        out_shape=jax.ShapeDtypeStruct(x.shape, F32),
        scratch_shapes=[pltpu.VMEM((ts, n_in), F32), pltpu.VMEM((ts, hw), F32),
                        pltpu.VMEM((ts, hw), BF16), pltpu.VMEM((ts, hw), BF16),
                        pltpu.VMEM((ts, 2 * hw), BF16), pltpu.VMEM((SUBLANES, 3 * hw), F32),
                        pltpu.VMEM((HEADS, HEAD_DIM, HEAD_DIM), F32), pltpu.VMEM((HEADS, HEAD_DIM, HEAD_DIM), F32)],
        compiler_params=pltpu.CompilerParams(
            dimension_semantics=("arbitrary", "arbitrary"), vmem_limit_bytes=VMEM_LIMIT_BYTES),
        name="odd_mixer",
    )(x, mod, row(norm_w), _bf(w_in_p), lb_logits.astype(F32), row(hg_norm_w), conv_w.astype(F32),
      lane_pad(a_log, 0), lane_pad(dt_bias, 0), row(gd_norm_w), _bf(w_out))


def _mlp_kernel(x_ref, mod_ref, nw_ref, w1_ref, w2_ref, fw_ref, o_ref, *, n_split, final):
    x = x_ref[0]
    shift, scale, gate = mod_ref[0, 3:4, :], mod_ref[0, 4:5, :], mod_ref[0, 5:6, :]
    h = _bf(_rmsnorm(x, nw_ref[...]) * (1.0 + scale) + shift)
    hidden = w1_ref.shape[1]
    step = hidden // n_split
    acc = jnp.zeros(x.shape, F32)
    for s in range(n_split):
        hid = jnp.square(jnp.maximum(_dot(h, w1_ref[:, s * step:(s + 1) * step]), 0.0))
        acc = acc + _dot(_bf(hid), w2_ref[s * step:(s + 1) * step, :])
    y = x + gate * acc
    if final:
        y = _rmsnorm(y, fw_ref[...])
    o_ref[0] = y


def _mlp(x, mod, norm_w, w1, w2, final_w, final, tm=512, n_split=4):
    bsz, seq, d = x.shape
    hidden = w1.shape[1]
    kern = functools.partial(_mlp_kernel, n_split=n_split, final=final)
    row = lambda v: v.reshape(1, -1).astype(F32)
    return pl.pallas_call(
        kern,
        grid=(bsz, seq // tm),
        in_specs=[pl.BlockSpec((1, tm, d), lambda b, j: (b, j, 0)),
                  pl.BlockSpec((1, 6, d), lambda b, j: (b, 0, 0)),
                  _const_spec((1, d)), _const_spec((d, hidden)), _const_spec((hidden, d)), _const_spec((1, d))],
        out_specs=pl.BlockSpec((1, tm, d), lambda b, j: (b, j, 0)),
        out_shape=jax.ShapeDtypeStruct(x.shape, F32),
        compiler_params=pltpu.CompilerParams(
            dimension_semantics=("arbitrary", "arbitrary"), vmem_limit_bytes=VMEM_LIMIT_BYTES),
        name="mlp_final" if final else "mlp",
    )(x, mod, row(norm_w), _bf(w1), _bf(w2), row(final_w))


def kernel(x, c, positions, ada_w, ada_b, norm_mix_w, norm_mlp_w, mlp_w1, mlp_w2, final_norm_w, ev_w_in, lru_conv_w, lru_conv_b, lru_w_a, lru_b_a, lru_w_x, lru_b_x, lru_lambda, ev_w_out, hg_lb_logits, od_w_in, hg_norm_w, gd_conv_w, gd_a_log, gd_dt_bias, gd_norm_w, od_w_out):
    depth = ada_w.shape[0]
    bsz, seq, d = x.shape
    mod = _adaln_mod(c, ada_w, ada_b).reshape(depth, bsz, 6, d)
    posf = positions.astype(F32)[..., None]
    for layer in range(depth):
        j = layer // 2
        if layer % 2 == 0:
            x = _even_mixer(x, mod[layer], posf, norm_mix_w[layer], ev_w_in[j], lru_conv_w[j], lru_conv_b[j],
                            lru_w_a[j], lru_b_a[j], lru_w_x[j], lru_b_x[j], lru_lambda[j], ev_w_out[j])
        else:
            x = _odd_mixer(x, mod[layer], norm_mix_w[layer], hg_lb_logits, layer, od_w_in[j], hg_norm_w[j],
                           gd_conv_w[j], gd_a_log[j], gd_dt_bias[j], gd_norm_w[j], od_w_out[j])
        x = _mlp(x, mod[layer], norm_mlp_w[layer], mlp_w1[layer], mlp_w2[layer], final_norm_w,
                 final=(layer == depth - 1))
    return x
```

```python
import functools
import math

import jax
import jax.numpy as jnp
from jax import lax
from jax.experimental import pallas as pl
from jax.experimental.pallas import tpu as pltpu

F32 = jnp.float32
BF16 = jnp.bfloat16

EPS = 1e-6
CONV_WIDTH = 4
LRU_C = 8.0
ROPE_BASE = 10000.0
HEADS = 4
HEAD_DIM = 128
HG_CHUNK = 32
GD_CHUNK = 64
GD_BASE = 16
SUBLANES = 8
LANES = 128
VMEM_LIMIT_BYTES = 56 * 1024 * 1024


def _dot(a, b):
    return jnp.dot(a, b, preferred_element_type=F32)


def _dot_nt(a, b):
    return lax.dot_general(a, b, (((1,), (1,)), ((), ())), preferred_element_type=F32)


def _dot_tn(a, b):
    return lax.dot_general(a, b, (((0,), (0,)), ((), ())), preferred_element_type=F32)


def _bf(x):
    return x.astype(BF16)


def _dot_exact_rhs(sel_bf16, x):
    hi = _bf(x)
    r1 = x - hi.astype(F32)
    mid = _bf(r1)
    lo = _bf(r1 - mid.astype(F32))
    return _dot(sel_bf16, hi) + _dot(sel_bf16, mid) + _dot(sel_bf16, lo)


def _softplus(x):
    return jnp.maximum(x, 0.0) + jnp.log1p(jnp.exp(-jnp.abs(x)))


def _neg_expm1(x):
    t = jnp.tanh(0.5 * x)
    return -2.0 * t / (1.0 - t)


def _silu(x):
    return x * jax.nn.sigmoid(x)


def _rmsnorm(x, w):
    return x * lax.rsqrt(jnp.mean(x * x, axis=-1, keepdims=True) + EPS) * w


def _shift_rows(x, prev8, s):
    rolled = pltpu.roll(x, s, axis=0)
    row = lax.broadcasted_iota(jnp.int32, (SUBLANES, x.shape[1]), 0)
    top = jnp.where(row < s, pltpu.roll(prev8, s, axis=0), rolled[0:SUBLANES])
    return jnp.concatenate([top, rolled[SUBLANES:]], axis=0)


def _causal_conv(x, prev8, w_ref, bias):
    acc = x * w_ref[CONV_WIDTH - 1:CONV_WIDTH, :]
    if bias is not None:
        acc = acc + bias
    for s in range(1, CONV_WIDTH):
        acc = acc + _shift_rows(x, prev8, s) * w_ref[CONV_WIDTH - 1 - s:CONV_WIDTH - s, :]
    return acc


def _same_block(n, blk):
    r = lax.broadcasted_iota(jnp.int32, (n, n), 0)
    c = lax.broadcasted_iota(jnp.int32, (n, n), 1)
    sh = int(math.log2(blk))
    return lax.shift_right_logical(r, sh) == lax.shift_right_logical(c, sh)


def _block_masks(n, blk):
    r = lax.broadcasted_iota(jnp.int32, (n, n), 0)
    c = lax.broadcasted_iota(jnp.int32, (n, n), 1)
    same = _same_block(n, blk)
    return same & (c <= r), same & (c < r)


def _unit_lower_inverses(a_mats, n, blk, base):
    r = lax.broadcasted_iota(jnp.int32, (n, n), 0)
    c = lax.broadcasted_iota(jnp.int32, (n, n), 1)
    inner = _same_block(n, base)
    pw = [jnp.where(inner, a, 0.0) for a in a_mats]
    t_inv = [jnp.where(r == c, 1.0, 0.0) - p for p in pw]
    for _ in range(int(math.log2(base)) - 1):
        pwb = [_bf(p) for p in pw]
        pw = [_dot(p, p) for p in pwb]
        t_inv = [t + _dot(_bf(t), _bf(p)) for t, p in zip(t_inv, pw)]
    size = base
    while size < blk:
        size *= 2
        outer = _same_block(n, size)
        off = [_bf(jnp.where(outer & jnp.logical_not(inner), a, 0.0)) for a in a_mats]
        tb = [_bf(t) for t in t_inv]
        t_inv = [t - _dot(b, _bf(_dot(o, b))) for t, b, o in zip(t_inv, tb, off)]
        inner = outer
    return t_inv


def _mod_kernel(c_ref, w_ref, b_ref, o_ref):
    c = c_ref[...]
    o_ref[0] = _dot(_bf(_silu(c)), _bf(w_ref[0])) + b_ref[0]


def _adaln_mod(c, ada_w, ada_b, tn=1536):
    depth, d, n = ada_w.shape
    b = c.shape[0]
    return pl.pallas_call(
        _mod_kernel,
        grid=(depth, n // tn),
        in_specs=[pl.BlockSpec((b, d), lambda l, j: (0, 0)),
                  pl.BlockSpec((1, d, tn), lambda l, j: (l, 0, j)),
                  pl.BlockSpec((1, 1, tn), lambda l, j: (l, 0, j))],
        out_specs=pl.BlockSpec((1, b, tn), lambda l, j: (l, 0, j)),
        out_shape=jax.ShapeDtypeStruct((depth, b, n), F32),
        compiler_params=pltpu.CompilerParams(
            dimension_semantics=("arbitrary", "arbitrary"), vmem_limit_bytes=VMEM_LIMIT_BYTES),
        name="adaln_mod",
    )(c, ada_w, ada_b.reshape(depth, 1, n))


def _lru_scan(a_s, u_s, h0, ts):
    width = a_s.shape[1]
    row = lax.broadcasted_iota(jnp.int32, (SUBLANES, width), 0)

    hprev = h0
    for g in range(ts // SUBLANES):
        rows = slice(g * SUBLANES, (g + 1) * SUBLANES)
        a8 = a_s[rows, :]
        u8 = u_s[rows, :]
        for d in (1, 2, 4):
            keep = row >= d
            a_sh = jnp.where(keep, pltpu.roll(a8, d, axis=0), 1.0)
            u_sh = jnp.where(keep, pltpu.roll(u8, d, axis=0), 0.0)
            u8 = u8 + a8 * u_sh
            a8 = a8 * a_sh
        h8 = u8 + a8 * hprev
        u_s[rows, :] = h8
        hprev = h8[SUBLANES - 1:SUBLANES, :]
    return hprev


def _even_kernel(x_ref, mod_ref, pos_ref, inv_ref, sgn_ref, nw_ref, win_ref, cw_ref, cb_ref,
                 wga_ref, wgx_ref, ba_ref, bx_ref, lam_ref, wout_ref, o_ref,
                 proj_s, a_s, u_s, mix_s, xprev_s, hprev_s, state_s, *, ts, chunk):
    lw = a_s.shape[1]
    half = lw // 2
    hd = HEAD_DIM

    @pl.when(pl.program_id(1) == 0)
    def _():
        xprev_s[...] = jnp.zeros_like(xprev_s)
        hprev_s[...] = jnp.zeros_like(hprev_s)
        state_s[...] = jnp.zeros_like(state_s)

    x = x_ref[0]
    shift, scale, gate = mod_ref[0, 0:1, :], mod_ref[0, 1:2, :], mod_ref[0, 2:3, :]
    h = _rmsnorm(x, nw_ref[...]) * (1.0 + scale) + shift
    proj_s[...] = _dot(_bf(h), win_ref[...])

    xr = proj_s[:, 0:lw]
    xc = _causal_conv(xr, xprev_s[...], cw_ref, cb_ref[...])
    xprev_s[...] = xr[ts - SUBLANES:ts, :]
    xcb = _bf(xc)
    r_pre = jnp.concatenate([_dot(xcb[:, :half], wga_ref[0]), _dot(xcb[:, half:], wga_ref[1])], axis=1)
    i_pre = jnp.concatenate([_dot(xcb[:, :half], wgx_ref[0]), _dot(xcb[:, half:], wgx_ref[1])], axis=1)
    r = jax.nn.sigmoid(r_pre + ba_ref[...])
    i = jax.nn.sigmoid(i_pre + bx_ref[...])
    log_a = (-LRU_C) * r * _softplus(-lam_ref[...])
    a_s[...] = jnp.exp(log_a)
    u_s[...] = jnp.sqrt(_neg_expm1(2.0 * log_a)) * (i * xc)
    hprev_s[...] = _lru_scan(a_s, u_s, hprev_s[...], ts)
    yr = proj_s[:, lw:2 * lw]
    mix_s[:, 0:lw] = _bf(u_s[...] * jax.nn.gelu(yr, approximate=True))

    pos = pos_ref[0]
    lane = lax.broadcasted_iota(jnp.int32, (ts // 2, hd), 1)
    first = lane < hd // 2
    ang = jnp.where(first, pos[:ts // 2], pos[ts // 2:]) * inv_ref[...]
    cos_p, sin_p = jnp.cos(ang), jnp.sin(ang)
    cos_r, sin_r = pltpu.roll(cos_p, hd // 2, axis=1), pltpu.roll(sin_p, hd // 2, axis=1)
    cos_d = jnp.concatenate([jnp.where(first, cos_p, cos_r), jnp.where(first, cos_r, cos_p)], axis=0)
    sin_d = jnp.concatenate([jnp.where(first, sin_p, sin_r), jnp.where(first, sin_r, sin_p)], axis=0)
    sin_d = sin_d * sgn_ref[...]
    ridx = lax.broadcasted_iota(jnp.int32, (chunk, chunk), 0)
    cidx = lax.broadcasted_iota(jnp.int32, (chunk, chunk), 1)
    rel = (ridx - cidx).astype(F32)
    tcol = lax.broadcasted_iota(jnp.int32, (chunk, 1), 0).astype(F32)
    qoff, koff, voff, goff = 2 * lw, 2 * lw + HEADS * hd, 2 * lw + 2 * HEADS * hd, 2 * lw + 3 * HEADS * hd
    log_gamma = [math.log1p(-(2.0 ** (-5.0 - hh))) for hh in range(HEADS)]
    dmats = [jnp.where(rel >= 0, jnp.exp(lg * jnp.maximum(rel, 0.0)), 0.0) for lg in log_gamma]
    q_decs = [jnp.exp(lg * (tcol + 1.0)) for lg in log_gamma]
    k_decs = [jnp.exp(lg * (chunk - 1.0 - tcol)) for lg in log_gamma]
    n_ret = ts // chunk
    pairs = [(cc, hh) for cc in range(n_ret) for hh in range(HEADS)]
    qb, vb, sc, kv = {}, {}, {}, {}
    for cc, hh in pairs:
        rows = slice(cc * chunk, (cc + 1) * chunk)
        cos_c, sin_c = cos_d[rows], sin_d[rows]
        qc = proj_s[rows, qoff + hh * hd:qoff + (hh + 1) * hd]
        kc = proj_s[rows, koff + hh * hd:koff + (hh + 1) * hd]
        qc = qc * cos_c + pltpu.roll(qc, hd // 2, axis=1) * sin_c
        kc = (kc * cos_c + pltpu.roll(kc, hd // 2, axis=1) * sin_c) * (hd ** -0.5)
        vb[cc, hh] = _bf(proj_s[rows, voff + hh * hd:voff + (hh + 1) * hd])
        qb[cc, hh] = _bf(qc)
        sc[cc, hh] = _bf(_dot_nt(qb[cc, hh], _bf(kc)) * dmats[hh])
        kv[cc, hh] = _dot_tn(_bf(kc * k_decs[hh]), vb[cc, hh])
    st_in = {}
    for hh, lg in enumerate(log_gamma):
        st = state_s[hh]
        for cc in range(n_ret):
            st_in[cc, hh] = _bf(st)
            st = math.exp(lg * chunk) * st + kv[cc, hh]
        state_s[hh] = st
    outs = {p: _dot(sc[p], vb[p]) + _dot(qb[p], st_in[p]) * q_decs[p[1]] for p in pairs}
    for cc, hh in pairs:
        rows = slice(cc * chunk, (cc + 1) * chunk)
        o = outs[cc, hh]
        mu = jnp.mean(o, axis=-1, keepdims=True)
        oc = o - mu
        var = jnp.mean(oc * oc, axis=-1, keepdims=True)
        g = proj_s[rows, goff + hh * hd:goff + (hh + 1) * hd]
        mix_s[rows, lw + hh * hd:lw + (hh + 1) * hd] = _bf(oc * lax.rsqrt(var + EPS) * _silu(g))

    o_ref[0] = x + gate * _dot(mix_s[...], wout_ref[...])


def _blockdiag_halves(w):
    n, bi, bj = w.shape
    eye = jnp.eye(n // 2, dtype=w.dtype)
    halves = []
    for p in range(2):
        blocks = w[p * (n // 2):(p + 1) * (n // 2)]
        dense = jnp.einsum('nm,nij->nimj', eye, blocks).reshape(n // 2 * bi, n // 2 * bj)
        halves.append(dense)
    return jnp.stack(halves)


def _const_spec(shape):
    nd = len(shape)
    return pl.BlockSpec(shape, lambda b, j: (0,) * nd)


def _even_mixer(x, mod, posf, norm_w, w_in, conv_w, conv_b, w_a, b_a, w_x, b_x, lam, w_out, ts=256, chunk=128):
    bsz, seq, d = x.shape
    lw = lam.shape[-1]
    n_in = w_in.shape[1]
    half = HEAD_DIM // 2
    inv = ROPE_BASE ** (-jnp.arange(half, dtype=F32) / half)
    inv2 = jnp.concatenate([inv, inv]).reshape(1, HEAD_DIM)
    sgn = jnp.concatenate([-jnp.ones((half,), F32), jnp.ones((half,), F32)]).reshape(1, HEAD_DIM)
    row = lambda v: v.reshape(1, -1).astype(F32)
    kern = functools.partial(_even_kernel, ts=ts, chunk=chunk)
    return pl.pallas_call(
        kern,
        grid=(bsz, seq // ts),
        in_specs=[pl.BlockSpec((1, ts, d), lambda b, j: (b, j, 0)),
                  pl.BlockSpec((1, 6, d), lambda b, j: (b, 0, 0)),
                  pl.BlockSpec((1, ts, 1), lambda b, j: (b, j, 0)),
                  _const_spec((1, HEAD_DIM)), _const_spec((1, HEAD_DIM)), _const_spec((1, d)),
                  _const_spec((d, n_in)), _const_spec((CONV_WIDTH, lw)), _const_spec((1, lw)),
                  _const_spec((2, lw // 2, lw // 2)), _const_spec((2, lw // 2, lw // 2)),
                  _const_spec((1, lw)), _const_spec((1, lw)), _const_spec((1, lw)),
                  _const_spec((lw + HEADS * HEAD_DIM, d))],
        out_specs=pl.BlockSpec((1, ts, d), lambda b, j: (b, j, 0)),
        out_shape=jax.ShapeDtypeStruct(x.shape, F32),
        scratch_shapes=[pltpu.VMEM((ts, n_in), F32), pltpu.VMEM((ts, lw), F32), pltpu.VMEM((ts, lw), F32),
                        pltpu.VMEM((ts, lw + HEADS * HEAD_DIM), BF16), pltpu.VMEM((SUBLANES, lw), F32),
                        pltpu.VMEM((1, lw), F32), pltpu.VMEM((HEADS, HEAD_DIM, HEAD_DIM), F32)],
        compiler_params=pltpu.CompilerParams(
            dimension_semantics=("arbitrary", "arbitrary"), vmem_limit_bytes=VMEM_LIMIT_BYTES),
        name="even_mixer",
    )(x, mod, posf, inv2, sgn, row(norm_w), _bf(w_in), conv_w.astype(F32), row(conv_b),
      _bf(_blockdiag_halves(w_a)), _bf(_blockdiag_halves(w_x)), row(b_a), row(b_x), row(lam), _bf(w_out))


def _odd_kernel(x_ref, mod_ref, nw_ref, win_ref, lbl_ref, hgw_ref, cw_ref, alog_ref, dtb_ref, gdw_ref,
                wout_ref, o_ref, proj_s, b_s, qi_s, ki_s, mix_s, cprev_s, hg_state_s, gd_state_s,
                *, ts, layer):
    hd = HEAD_DIM
    hw = HEADS * hd

    @pl.when(pl.program_id(1) == 0)
    def _():
        cprev_s[...] = jnp.zeros_like(cprev_s)
        hg_state_s[...] = jnp.zeros_like(hg_state_s)
        gd_state_s[...] = jnp.zeros_like(gd_state_s)

    x = x_ref[0]
    shift, scale, gate = mod_ref[0, 0:1, :], mod_ref[0, 1:2, :], mod_ref[0, 2:3, :]
    h = _rmsnorm(x, nw_ref[...]) * (1.0 + scale) + shift
    proj_s[...] = _dot(_bf(h), win_ref[...])

    logits = lbl_ref[...]
    e = jnp.exp(logits - jnp.max(logits, axis=0, keepdims=True))
    lb = jnp.sum(e[1:layer + 1], axis=0, keepdims=True) / jnp.sum(e, axis=0, keepdims=True)
    key = (1.0 - lb) * jax.nn.sigmoid(-proj_s[:, hw:2 * hw])
    log_f = jnp.log1p(-key)
    incl32, _ = _block_masks(ts, HG_CHUNK)
    b_s[...] = _dot_exact_rhs(jnp.where(incl32, 1.0, 0.0).astype(BF16), log_f)
    mid = HG_CHUNK // 2
    n_hg = ts // HG_CHUNK
    for cc in range(n_hg):
        rows = slice(cc * HG_CHUNK, (cc + 1) * HG_CHUNK)
        b = b_s[rows, :]
        b_mid = b[mid:mid + 1, :]
        qi_s[rows, :] = _bf(proj_s[rows, 0:hw] * jnp.exp(b - b_mid))
        ki_s[rows, :] = _bf(key[rows] * jnp.exp(b_mid - b))
    head_cols = [slice(hh * hd, (hh + 1) * hd) for hh in range(HEADS)]
    hg_v = [_bf(proj_s[:, 2 * hw + hh * hd:2 * hw + (hh + 1) * hd]) for hh in range(HEADS)]
    hg_scores = [jnp.where(incl32, _dot_nt(qi_s[:, cols], ki_s[:, cols]), 0.0) for cols in head_cols]
    hg_intra = [_dot(_bf(sc), v) for sc, v in zip(hg_scores, hg_v)]
    hg_pairs = [(cc, hh) for cc in range(n_hg) for hh in range(HEADS)]
    hg_rows = [slice(cc * HG_CHUNK, (cc + 1) * HG_CHUNK) for cc in range(n_hg)]
    hg_kv, hg_dec, hg_qs = {}, {}, {}
    for cc, hh in hg_pairs:
        rows, cols = hg_rows[cc], head_cols[hh]
        b = b_s[rows, cols]
        b_last = b[HG_CHUNK - 1:HG_CHUNK, :]
        hg_dec[cc, hh] = jnp.exp(b_last)
        hg_qs[cc, hh] = _bf(proj_s[rows, cols] * jnp.exp(b))
        hg_kv[cc, hh] = _dot_tn(hg_v[hh][rows], _bf(key[rows, cols] * jnp.exp(b_last - b)))
    hg_in = {}
    for hh in range(HEADS):
        st_t = hg_state_s[hh]
        for cc in range(n_hg):
            hg_in[cc, hh] = _bf(st_t)
            st_t = st_t * hg_dec[cc, hh] + hg_kv[cc, hh]
        hg_state_s[hh] = st_t
    hg_out = {p: hg_intra[p[1]][hg_rows[p[0]]] + _dot_nt(hg_qs[p], hg_in[p]) for p in hg_pairs}
    for cc, hh in hg_pairs:
        rows = hg_rows[cc]
        o = hg_out[cc, hh]
        on = o * lax.rsqrt(jnp.mean(o * o, axis=-1, keepdims=True) + EPS) * hgw_ref[...]
        gh = proj_s[rows, 3 * hw + hh * hd:3 * hw + (hh + 1) * hd]
        mix_s[rows, head_cols[hh]] = _bf(on * _silu(gh))

    qkv_raw = proj_s[:, 4 * hw:7 * hw]
    qkv = _silu(_causal_conv(qkv_raw, cprev_s[...], cw_ref, None))
    cprev_s[...] = qkv_raw[ts - SUBLANES:ts, :]
    ab = proj_s[:, 8 * hw:8 * hw + LANES]
    g_all = -jnp.exp(alog_ref[...]) * _softplus(ab + dtb_ref[...])
    beta_all = jax.nn.sigmoid(ab)
    incl64, strict64 = _block_masks(ts, GD_CHUNK)
    gcum = _dot_exact_rhs(jnp.where(incl64, 1.0, 0.0).astype(BF16), g_all)
    gcum_t = gcum.T
    n_gd = ts // GD_CHUNK
    gd_q, gd_k, gd_gcol, gd_qk, a_mats, rhs_all = [], [], [], [], [], []
    for hh in range(HEADS):
        qh = qkv[:, hh * hd:(hh + 1) * hd]
        kh = qkv[:, hw + hh * hd:hw + (hh + 1) * hd]
        vh = qkv[:, 2 * hw + hh * hd:2 * hw + (hh + 1) * hd]
        qh = qh * lax.rsqrt(jnp.sum(qh * qh, axis=-1, keepdims=True) + EPS) * (hd ** -0.5)
        kh = kh * lax.rsqrt(jnp.sum(kh * kh, axis=-1, keepdims=True) + EPS)
        gcol = gcum[:, hh:hh + 1]
        grow = gcum_t[hh:hh + 1, :]
        beta = beta_all[:, HEADS + hh:HEADS + hh + 1]
        decay = jnp.where(incl64, jnp.exp(jnp.where(incl64, gcol - grow, 0.0)), 0.0)
        kb = kh * beta
        khb = _bf(kh)
        a_mats.append(jnp.where(strict64, _dot_nt(_bf(kb), khb) * decay, 0.0))
        gd_qk.append(_bf(jnp.where(incl64, _dot_nt(_bf(qh), khb) * decay, 0.0)))
        egc = jnp.exp(gcol)
        rhs_all.append(_bf(jnp.concatenate([vh * beta, kb * egc], axis=1)))
        gd_q.append(_bf(qh * egc))
        gd_k.append(kh)
        gd_gcol.append(gcol)
    t_inv = _unit_lower_inverses(a_mats, ts, GD_CHUNK, GD_BASE)
    sol = [_dot(_bf(t), rh) for t, rh in zip(t_inv, rhs_all)]
    chunk_rows = [slice(cc * GD_CHUNK, (cc + 1) * GD_CHUNK) for cc in range(n_gd)]
    gd_states = [[None] * n_gd for _ in range(HEADS)]
    ktu = [[None] * n_gd for _ in range(HEADS)]
    ktw = [[None] * n_gd for _ in range(HEADS)]
    g_last = [[None] * n_gd for _ in range(HEADS)]
    for cc, rows in enumerate(chunk_rows):
        for hh in range(HEADS):
            gcol = gd_gcol[hh]
            g_last[hh][cc] = gcol[(cc + 1) * GD_CHUNK - 1:(cc + 1) * GD_CHUNK, :]
            k_g = _bf(gd_k[hh][rows] * jnp.exp(g_last[hh][cc] - gcol[rows]))
            ktu[hh][cc] = _dot_tn(k_g, _bf(sol[hh][rows, :hd]))
            ktw[hh][cc] = _bf(_dot_tn(k_g, _bf(sol[hh][rows, hd:])))
    st = [gd_state_s[hh] for hh in range(HEADS)]
    for cc in range(n_gd):
        for hh in range(HEADS):
            gd_states[hh][cc] = _bf(st[hh])
            st[hh] = st[hh] * jnp.exp(g_last[hh][cc]) + ktu[hh][cc] - _dot(ktw[hh][cc], gd_states[hh][cc])
    for hh in range(HEADS):
        gd_state_s[hh] = st[hh]
    gd_pairs = [(cc, hh) for cc in range(n_gd) for hh in range(HEADS)]
    wq = {(cc, hh): _dot(jnp.concatenate([_bf(sol[hh][chunk_rows[cc], hd:]), gd_q[hh][chunk_rows[cc]]], axis=0),
                         gd_states[hh][cc]) for cc, hh in gd_pairs}
    v_newb = {(cc, hh): _bf(sol[hh][chunk_rows[cc], :hd] - wq[cc, hh][:GD_CHUNK]) for cc, hh in gd_pairs}
    gd_out = {(cc, hh): wq[cc, hh][GD_CHUNK:] + _dot(gd_qk[hh][chunk_rows[cc], chunk_rows[cc]], v_newb[cc, hh])
              for cc, hh in gd_pairs}
    for cc, hh in gd_pairs:
        rows = chunk_rows[cc]
        o = gd_out[cc, hh]
        on = o * lax.rsqrt(jnp.mean(o * o, axis=-1, keepdims=True) + EPS) * gdw_ref[...]
        zh = proj_s[rows, 7 * hw + hh * hd:7 * hw + (hh + 1) * hd]
        mix_s[rows, hw + hh * hd:hw + (hh + 1) * hd] = _bf(on * _silu(zh))

    o_ref[0] = x + gate * _dot(mix_s[...], wout_ref[...])


def _odd_mixer(x, mod, norm_w, lb_logits, layer, w_in, hg_norm_w, conv_w, a_log, dt_bias, gd_norm_w, w_out, ts=256):
    bsz, seq, d = x.shape
    hw = HEADS * HEAD_DIM
    n_in = 8 * hw + LANES
    pad = n_in - w_in.shape[1]
    w_in_p = jnp.concatenate([w_in, jnp.zeros((d, pad), w_in.dtype)], axis=1)
    lane_pad = lambda v, off: jnp.zeros((1, LANES), F32).at[0, off:off + v.shape[0]].set(v.astype(F32))
    row = lambda v: v.reshape(1, -1).astype(F32)
    depth = lb_logits.shape[0]
    kern = functools.partial(_odd_kernel, ts=ts, layer=layer)
    return pl.pallas_call(
        kern,
        grid=(bsz, seq // ts),
        in_specs=[pl.BlockSpec((1, ts, d), lambda b, j: (b, j, 0)),
                  pl.BlockSpec((1, 6, d), lambda b, j: (b, 0, 0)),
                  _const_spec((1, d)), _const_spec((d, n_in)), _const_spec((depth, hw)),
                  _const_spec((1, HEAD_DIM)), _const_spec((CONV_WIDTH, 3 * hw)),
                  _const_spec((1, LANES)), _const_spec((1, LANES)), _const_spec((1, HEAD_DIM)),
                  _const_spec((2 * hw, d))],
        out_specs=pl.BlockSpec((1, ts, d), lambda b, j: (b, j, 0)),
        out_shape=jax.ShapeDtypeStruct(x.shape, F32),
        scratch_shapes=[pltpu.VMEM((ts, n_in), F32), pltpu.VMEM((ts, hw), F32),
                        pltpu.VMEM((ts, hw), BF16), pltpu.VMEM((ts, hw), BF16),
                        pltpu.VMEM((ts, 2 * hw), BF16), pltpu.VMEM((SUBLANES, 3 * hw), F32),
                        pltpu.VMEM((HEADS, HEAD_DIM, HEAD_DIM), F32), pltpu.VMEM((HEADS, HEAD_DIM, HEAD_DIM), F32)],
        compiler_params=pltpu.CompilerParams(
            dimension_semantics=("arbitrary", "arbitrary"), vmem_limit_bytes=VMEM_LIMIT_BYTES),
        name="odd_mixer",
    )(x, mod, row(norm_w), _bf(w_in_p), lb_logits.astype(F32), row(hg_norm_w), conv_w.astype(F32),
      lane_pad(a_log, 0), lane_pad(dt_bias, 0), row(gd_norm_w), _bf(w_out))


def _mlp_kernel(x_ref, mod_ref, nw_ref, w1_ref, w2_ref, fw_ref, o_ref, *, n_split, final):
    x = x_ref[0]
    shift, scale, gate = mod_ref[0, 3:4, :], mod_ref[0, 4:5, :], mod_ref[0, 5:6, :]
    h = _bf(_rmsnorm(x, nw_ref[...]) * (1.0 + scale) + shift)
    hidden = w1_ref.shape[1]
    step = hidden // n_split
    acc = jnp.zeros(x.shape, F32)
    for s in range(n_split):
        hid = jnp.square(jnp.maximum(_dot(h, w1_ref[:, s * step:(s + 1) * step]), 0.0))
        acc = acc + _dot(_bf(hid), w2_ref[s * step:(s + 1) * step, :])
    y = x + gate * acc
    if final:
        y = _rmsnorm(y, fw_ref[...])
    o_ref[0] = y


def _mlp(x, mod, norm_w, w1, w2, final_w, final, tm=512, n_split=4):
    bsz, seq, d = x.shape
    hidden = w1.shape[1]
    kern = functools.partial(_mlp_kernel, n_split=n_split, final=final)
    row = lambda v: v.reshape(1, -1).astype(F32)
    return pl.pallas_call(
        kern,
        grid=(bsz, seq // tm),
        in_specs=[pl.BlockSpec((1, tm, d), lambda b, j: (b, j, 0)),
                  pl.BlockSpec((1, 6, d), lambda b, j: (b, 0, 0)),
                  _const_spec((1, d)), _const_spec((d, hidden)), _const_spec((hidden, d)), _const_spec((1, d))],
        out_specs=pl.BlockSpec((1, tm, d), lambda b, j: (b, j, 0)),
        out_shape=jax.ShapeDtypeStruct(x.shape, F32),
        compiler_params=pltpu.CompilerParams(
            dimension_semantics=("arbitrary", "arbitrary"), vmem_limit_bytes=VMEM_LIMIT_BYTES),
        name="mlp_final" if final else "mlp",
    )(x, mod, row(norm_w), _bf(w1), _bf(w2), row(final_w))


def kernel(x, c, positions, ada_w, ada_b, norm_mix_w, norm_mlp_w, mlp_w1, mlp_w2, final_norm_w, ev_w_in, lru_conv_w, lru_conv_b, lru_w_a, lru_b_a, lru_w_x, lru_b_x, lru_lambda, ev_w_out, hg_lb_logits, od_w_in, hg_norm_w, gd_conv_w, gd_a_log, gd_dt_bias, gd_norm_w, od_w_out):
    depth = ada_w.shape[0]
    bsz, seq, d = x.shape
    mod = _adaln_mod(c, ada_w, ada_b).reshape(depth, bsz, 6, d)
    posf = positions.astype(F32)[..., None]
    for layer in range(depth):
        j = layer // 2
        if layer % 2 == 0:
            x = _even_mixer(x, mod[layer], posf, norm_mix_w[layer], ev_w_in[j], lru_conv_w[j], lru_conv_b[j],
                            lru_w_a[j], lru_b_a[j], lru_w_x[j], lru_b_x[j], lru_lambda[j], ev_w_out[j])
        else:
            x = _odd_mixer(x, mod[layer], norm_mix_w[layer], hg_lb_logits, layer, od_w_in[j], hg_norm_w[j],
                           gd_conv_w[j], gd_a_log[j], gd_dt_bias[j], gd_norm_w[j], od_w_out[j])
        x = _mlp(x, mod[layer], norm_mlp_w[layer], mlp_w1[layer], mlp_w2[layer], final_norm_w,
                 final=(layer == depth - 1))
    return x
```

```python
import functools
import math

import jax
import jax.numpy as jnp
from jax import lax
from jax.experimental import pallas as pl
from jax.experimental.pallas import tpu as pltpu

F32 = jnp.float32
BF16 = jnp.bfloat16

EPS = 1e-6
CONV_WIDTH = 4
LRU_C = 8.0
ROPE_BASE = 10000.0
HEADS = 4
HEAD_DIM = 128
HG_CHUNK = 32
GD_CHUNK = 64
GD_BASE = 16
SUBLANES = 8
LANES = 128
VMEM_LIMIT_BYTES = 56 * 1024 * 1024


def _dot(a, b):
    return jnp.dot(a, b, preferred_element_type=F32)


def _dot_nt(a, b):
    return lax.dot_general(a, b, (((1,), (1,)), ((), ())), preferred_element_type=F32)


def _dot_tn(a, b):
    return lax.dot_general(a, b, (((0,), (0,)), ((), ())), preferred_element_type=F32)


def _bf(x):
    return x.astype(BF16)


def _dot_exact_rhs(sel_bf16, x):
    hi = _bf(x)
    r1 = x - hi.astype(F32)
    mid = _bf(r1)
    lo = _bf(r1 - mid.astype(F32))
    return _dot(sel_bf16, hi) + _dot(sel_bf16, mid) + _dot(sel_bf16, lo)


def _softplus(x):
    return jnp.maximum(x, 0.0) + jnp.log1p(jnp.exp(-jnp.abs(x)))


def _neg_expm1(x):
    t = jnp.tanh(0.5 * x)
    return -2.0 * t / (1.0 - t)


def _silu(x):
    return x * jax.nn.sigmoid(x)


def _rmsnorm(x, w):
    return x * lax.rsqrt(jnp.mean(x * x, axis=-1, keepdims=True) + EPS) * w


def _shift_rows(x, prev8, s):
    rolled = pltpu.roll(x, s, axis=0)
    row = lax.broadcasted_iota(jnp.int32, (SUBLANES, x.shape[1]), 0)
    top = jnp.where(row < s, pltpu.roll(prev8, s, axis=0), rolled[0:SUBLANES])
    return jnp.concatenate([top, rolled[SUBLANES:]], axis=0)


def _causal_conv(x, prev8, w_ref, bias):
    acc = x * w_ref[CONV_WIDTH - 1:CONV_WIDTH, :]
    if bias is not None:
        acc = acc + bias
    for s in range(1, CONV_WIDTH):
        acc = acc + _shift_rows(x, prev8, s) * w_ref[CONV_WIDTH - 1 - s:CONV_WIDTH - s, :]
    return acc


def _same_block(n, blk):
    r = lax.broadcasted_iota(jnp.int32, (n, n), 0)
    c = lax.broadcasted_iota(jnp.int32, (n, n), 1)
    sh = int(math.log2(blk))
    return lax.shift_right_logical(r, sh) == lax.shift_right_logical(c, sh)


def _block_masks(n, blk):
    r = lax.broadcasted_iota(jnp.int32, (n, n), 0)
    c = lax.broadcasted_iota(jnp.int32, (n, n), 1)
    same = _same_block(n, blk)
    return same & (c <= r), same & (c < r)


def _unit_lower_inverses(a_mats, n, blk, base):
    r = lax.broadcasted_iota(jnp.int32, (n, n), 0)
    c = lax.broadcasted_iota(jnp.int32, (n, n), 1)
    inner = _same_block(n, base)
    pw = [jnp.where(inner, a, 0.0) for a in a_mats]
    t_inv = [jnp.where(r == c, 1.0, 0.0) - p for p in pw]
    for _ in range(int(math.log2(base)) - 1):
        pwb = [_bf(p) for p in pw]
        pw = [_dot(p, p) for p in pwb]
        t_inv = [t + _dot(_bf(t), _bf(p)) for t, p in zip(t_inv, pw)]
    size = base
    while size < blk:
        size *= 2
        outer = _same_block(n, size)
        off = [_bf(jnp.where(outer & jnp.logical_not(inner), a, 0.0)) for a in a_mats]
        tb = [_bf(t) for t in t_inv]
        t_inv = [t - _dot(b, _bf(_dot(o, b))) for t, b, o in zip(t_inv, tb, off)]
        inner = outer
    return t_inv


def _mod_kernel(c_ref, w_ref, b_ref, o_ref):
    c = c_ref[...]
    o_ref[0] = _dot(_bf(_silu(c)), _bf(w_ref[0])) + b_ref[0]


def _adaln_mod(c, ada_w, ada_b, tn=1536):
    depth, d, n = ada_w.shape
    b = c.shape[0]
    return pl.pallas_call(
        _mod_kernel,
        grid=(depth, n // tn),
        in_specs=[pl.BlockSpec((b, d), lambda l, j: (0, 0)),
                  pl.BlockSpec((1, d, tn), lambda l, j: (l, 0, j)),
                  pl.BlockSpec((1, 1, tn), lambda l, j: (l, 0, j))],
        out_specs=pl.BlockSpec((1, b, tn), lambda l, j: (l, 0, j)),
        out_shape=jax.ShapeDtypeStruct((depth, b, n), F32),
        compiler_params=pltpu.CompilerParams(
            dimension_semantics=("arbitrary", "arbitrary"), vmem_limit_bytes=VMEM_LIMIT_BYTES),
        name="adaln_mod",
    )(c, ada_w, ada_b.reshape(depth, 1, n))


def _lru_scan(a_s, u_s, h0, ts):
    width = a_s.shape[1]
    row = lax.broadcasted_iota(jnp.int32, (SUBLANES, width), 0)

    hprev = h0
    for g in range(ts // SUBLANES):
        rows = slice(g * SUBLANES, (g + 1) * SUBLANES)
        a8 = a_s[rows, :]
        u8 = u_s[rows, :]
        for d in (1, 2, 4):
            keep = row >= d
            a_sh = jnp.where(keep, pltpu.roll(a8, d, axis=0), 1.0)
            u_sh = jnp.where(keep, pltpu.roll(u8, d, axis=0), 0.0)
            u8 = u8 + a8 * u_sh
            a8 = a8 * a_sh
        h8 = u8 + a8 * hprev
        u_s[rows, :] = h8
        hprev = h8[SUBLANES - 1:SUBLANES, :]
    return hprev


def _even_kernel(x_ref, mod_ref, pos_ref, inv_ref, sgn_ref, nw_ref, win_ref, cw_ref, cb_ref,
                 wga_ref, wgx_ref, ba_ref, bx_ref, lam_ref, wout_ref, o_ref,
                 proj_all, a_s, u_s, mix_all, xprev_s, hprev_s, state_s, *, ts, chunk):
    lw = a_s.shape[1]
    half = lw // 2
    hd = HEAD_DIM
    n_sub = proj_all.shape[0] // ts

    def mix_tile(proj_s, mix_s, pos_view):
        xr = proj_s[:, 0:lw]
        xc = _causal_conv(xr, xprev_s[...], cw_ref, cb_ref[...])
        xprev_s[...] = xr[ts - SUBLANES:ts, :]
        xcb = _bf(xc)
        r_pre = jnp.concatenate([_dot(xcb[:, :half], wga_ref[0]), _dot(xcb[:, half:], wga_ref[1])], axis=1)
        i_pre = jnp.concatenate([_dot(xcb[:, :half], wgx_ref[0]), _dot(xcb[:, half:], wgx_ref[1])], axis=1)
        r = jax.nn.sigmoid(r_pre + ba_ref[...])
        i = jax.nn.sigmoid(i_pre + bx_ref[...])
        log_a = (-LRU_C) * r * _softplus(-lam_ref[...])
        a_s[...] = jnp.exp(log_a)
        u_s[...] = jnp.sqrt(_neg_expm1(2.0 * log_a)) * (i * xc)
        hprev_s[...] = _lru_scan(a_s, u_s, hprev_s[...], ts)
        yr = proj_s[:, lw:2 * lw]
        mix_s[:, 0:lw] = _bf(u_s[...] * jax.nn.gelu(yr, approximate=True))

        pos = pos_view[...]
        lane = lax.broadcasted_iota(jnp.int32, (ts // 2, hd), 1)
        first = lane < hd // 2
        ang = jnp.where(first, pos[:ts // 2], pos[ts // 2:]) * inv_ref[...]
        cos_p, sin_p = jnp.cos(ang), jnp.sin(ang)
        cos_r, sin_r = pltpu.roll(cos_p, hd // 2, axis=1), pltpu.roll(sin_p, hd // 2, axis=1)
        cos_d = jnp.concatenate([jnp.where(first, cos_p, cos_r), jnp.where(first, cos_r, cos_p)], axis=0)
        sin_d = jnp.concatenate([jnp.where(first, sin_p, sin_r), jnp.where(first, sin_r, sin_p)], axis=0)
        sin_d = sin_d * sgn_ref[...]
        ridx = lax.broadcasted_iota(jnp.int32, (chunk, chunk), 0)
        cidx = lax.broadcasted_iota(jnp.int32, (chunk, chunk), 1)
        rel = (ridx - cidx).astype(F32)
        tcol = lax.broadcasted_iota(jnp.int32, (chunk, 1), 0).astype(F32)
        qoff, koff, voff, goff = 2 * lw, 2 * lw + HEADS * hd, 2 * lw + 2 * HEADS * hd, 2 * lw + 3 * HEADS * hd
        log_gamma = [math.log1p(-(2.0 ** (-5.0 - hh))) for hh in range(HEADS)]
        dmats = [jnp.where(rel >= 0, jnp.exp(lg * jnp.maximum(rel, 0.0)), 0.0) for lg in log_gamma]
        q_decs = [jnp.exp(lg * (tcol + 1.0)) for lg in log_gamma]
        k_decs = [jnp.exp(lg * (chunk - 1.0 - tcol)) for lg in log_gamma]
        n_ret = ts // chunk
        pairs = [(cc, hh) for cc in range(n_ret) for hh in range(HEADS)]
        qb, vb, sc, kv = {}, {}, {}, {}
        for cc, hh in pairs:
            rows = slice(cc * chunk, (cc + 1) * chunk)
            cos_c, sin_c = cos_d[rows], sin_d[rows]
            qc = proj_s[rows, qoff + hh * hd:qoff + (hh + 1) * hd]
            kc = proj_s[rows, koff + hh * hd:koff + (hh + 1) * hd]
            qc = qc * cos_c + pltpu.roll(qc, hd // 2, axis=1) * sin_c
            kc = (kc * cos_c + pltpu.roll(kc, hd // 2, axis=1) * sin_c) * (hd ** -0.5)
            vb[cc, hh] = _bf(proj_s[rows, voff + hh * hd:voff + (hh + 1) * hd])
            qb[cc, hh] = _bf(qc)
            sc[cc, hh] = _bf(_dot_nt(qb[cc, hh], _bf(kc)) * dmats[hh])
            kv[cc, hh] = _dot_tn(_bf(kc * k_decs[hh]), vb[cc, hh])
        st_in = {}
        for hh, lg in enumerate(log_gamma):
            st = state_s[hh]
            for cc in range(n_ret):
                st_in[cc, hh] = _bf(st)
                st = math.exp(lg * chunk) * st + kv[cc, hh]
            state_s[hh] = st
        outs = {p: _dot(sc[p], vb[p]) + _dot(qb[p], st_in[p]) * q_decs[p[1]] for p in pairs}
        for cc, hh in pairs:
            rows = slice(cc * chunk, (cc + 1) * chunk)
            o = outs[cc, hh]
            mu = jnp.mean(o, axis=-1, keepdims=True)
            oc = o - mu
            var = jnp.mean(oc * oc, axis=-1, keepdims=True)
            g = proj_s[rows, goff + hh * hd:goff + (hh + 1) * hd]
            mix_s[rows, lw + hh * hd:lw + (hh + 1) * hd] = _bf(oc * lax.rsqrt(var + EPS) * _silu(g))

    @pl.when(pl.program_id(1) == 0)
    def _():
        xprev_s[...] = jnp.zeros_like(xprev_s)
        hprev_s[...] = jnp.zeros_like(hprev_s)
        state_s[...] = jnp.zeros_like(state_s)

    x = x_ref[0]
    shift, scale, gate = mod_ref[0, 0:1, :], mod_ref[0, 1:2, :], mod_ref[0, 2:3, :]
    h = _rmsnorm(x, nw_ref[...]) * (1.0 + scale) + shift
    proj_all[...] = _dot(_bf(h), win_ref[...])
    for t in range(n_sub):
        rows = pl.ds(t * ts, ts)
        mix_tile(proj_all.at[rows], mix_all.at[rows], pos_ref.at[0, rows])
    o_ref[0] = x + gate * _dot(mix_all[...], wout_ref[...])


def _blockdiag_halves(w):
    n, bi, bj = w.shape
    eye = jnp.eye(n // 2, dtype=w.dtype)
    halves = []
    for p in range(2):
        blocks = w[p * (n // 2):(p + 1) * (n // 2)]
        dense = jnp.einsum('nm,nij->nimj', eye, blocks).reshape(n // 2 * bi, n // 2 * bj)
        halves.append(dense)
    return jnp.stack(halves)


def _const_spec(shape):
    nd = len(shape)
    return pl.BlockSpec(shape, lambda *_: (0,) * nd, pipeline_mode=pl.Buffered(1))


def _even_mixer(x, mod, posf, norm_w, w_in, conv_w, conv_b, w_a, b_a, w_x, b_x, lam, w_out,
                ts=256, n_sub=4, chunk=128):
    bsz, seq, d = x.shape
    lw = lam.shape[-1]
    n_in = w_in.shape[1]
    half = HEAD_DIM // 2
    inv = ROPE_BASE ** (-jnp.arange(half, dtype=F32) / half)
    inv2 = jnp.concatenate([inv, inv]).reshape(1, HEAD_DIM)
    sgn = jnp.concatenate([-jnp.ones((half,), F32), jnp.ones((half,), F32)]).reshape(1, HEAD_DIM)
    row = lambda v: v.reshape(1, -1).astype(F32)
    rows = n_sub * ts
    kern = functools.partial(_even_kernel, ts=ts, chunk=chunk)
    return pl.pallas_call(
        kern,
        grid=(bsz, seq // rows),
        in_specs=[pl.BlockSpec((1, rows, d), lambda b, j: (b, j, 0)),
                  pl.BlockSpec((1, 6, d), lambda b, j: (b, 0, 0)),
                  pl.BlockSpec((1, rows, 1), lambda b, j: (b, j, 0)),
                  _const_spec((1, HEAD_DIM)), _const_spec((1, HEAD_DIM)), _const_spec((1, d)),
                  _const_spec((d, n_in)), _const_spec((CONV_WIDTH, lw)), _const_spec((1, lw)),
                  _const_spec((2, lw // 2, lw // 2)), _const_spec((2, lw // 2, lw // 2)),
                  _const_spec((1, lw)), _const_spec((1, lw)), _const_spec((1, lw)),
                  _const_spec((lw + HEADS * HEAD_DIM, d))],
        out_specs=pl.BlockSpec((1, rows, d), lambda b, j: (b, j, 0)),
        out_shape=jax.ShapeDtypeStruct(x.shape, F32),
        scratch_shapes=[pltpu.VMEM((rows, n_in), F32), pltpu.VMEM((ts, lw), F32), pltpu.VMEM((ts, lw), F32),
                        pltpu.VMEM((rows, lw + HEADS * HEAD_DIM), BF16), pltpu.VMEM((SUBLANES, lw), F32),
                        pltpu.VMEM((1, lw), F32), pltpu.VMEM((HEADS, HEAD_DIM, HEAD_DIM), F32)],
        compiler_params=pltpu.CompilerParams(
            dimension_semantics=("arbitrary", "arbitrary"), vmem_limit_bytes=VMEM_LIMIT_BYTES),
        name="even_mixer",
    )(x, mod, posf, inv2, sgn, row(norm_w), _bf(w_in), conv_w.astype(F32), row(conv_b),
      _bf(_blockdiag_halves(w_a)), _bf(_blockdiag_halves(w_x)), row(b_a), row(b_x), row(lam), _bf(w_out))


def _odd_kernel(x_ref, mod_ref, nw_ref, win_ref, lbl_ref, hgw_ref, cw_ref, alog_ref, dtb_ref, gdw_ref,
                wout_ref, o_ref, proj_all, b_s, qi_s, ki_s, mix_all, cprev_s, hg_state_s, gd_state_s,
                *, ts, layer):
    hd = HEAD_DIM
    hw = HEADS * hd
    n_sub = proj_all.shape[0] // ts

    def mix_tile(proj_s, mix_s):
        logits = lbl_ref[...]
        e = jnp.exp(logits - jnp.max(logits, axis=0, keepdims=True))
        lb = jnp.sum(e[1:layer + 1], axis=0, keepdims=True) / jnp.sum(e, axis=0, keepdims=True)
        key = (1.0 - lb) * jax.nn.sigmoid(-proj_s[:, hw:2 * hw])
        log_f = jnp.log1p(-key)
        incl32, _ = _block_masks(ts, HG_CHUNK)
        b_s[...] = _dot_exact_rhs(jnp.where(incl32, 1.0, 0.0).astype(BF16), log_f)
        mid = HG_CHUNK // 2
        n_hg = ts // HG_CHUNK
        for cc in range(n_hg):
            rows = slice(cc * HG_CHUNK, (cc + 1) * HG_CHUNK)
            b = b_s[rows, :]
            b_mid = b[mid:mid + 1, :]
            qi_s[rows, :] = _bf(proj_s[rows, 0:hw] * jnp.exp(b - b_mid))
            ki_s[rows, :] = _bf(key[rows] * jnp.exp(b_mid - b))
        head_cols = [slice(hh * hd, (hh + 1) * hd) for hh in range(HEADS)]
        hg_v = [_bf(proj_s[:, 2 * hw + hh * hd:2 * hw + (hh + 1) * hd]) for hh in range(HEADS)]
        hg_scores = [jnp.where(incl32, _dot_nt(qi_s[:, cols], ki_s[:, cols]), 0.0) for cols in head_cols]
        hg_intra = [_dot(_bf(sc), v) for sc, v in zip(hg_scores, hg_v)]
        hg_pairs = [(cc, hh) for cc in range(n_hg) for hh in range(HEADS)]
        hg_rows = [slice(cc * HG_CHUNK, (cc + 1) * HG_CHUNK) for cc in range(n_hg)]
        hg_kv, hg_dec, hg_qs = {}, {}, {}
        for cc, hh in hg_pairs:
            rows, cols = hg_rows[cc], head_cols[hh]
            b = b_s[rows, cols]
            b_last = b[HG_CHUNK - 1:HG_CHUNK, :]
            hg_dec[cc, hh] = jnp.exp(b_last)
            hg_qs[cc, hh] = _bf(proj_s[rows, cols] * jnp.exp(b))
            hg_kv[cc, hh] = _dot_tn(hg_v[hh][rows], _bf(key[rows, cols] * jnp.exp(b_last - b)))
        hg_in = {}
        for hh in range(HEADS):
            st_t = hg_state_s[hh]
            for cc in range(n_hg):
                hg_in[cc, hh] = _bf(st_t)
                st_t = st_t * hg_dec[cc, hh] + hg_kv[cc, hh]
            hg_state_s[hh] = st_t
        hg_out = {p: hg_intra[p[1]][hg_rows[p[0]]] + _dot_nt(hg_qs[p], hg_in[p]) for p in hg_pairs}
        for cc, hh in hg_pairs:
            rows = hg_rows[cc]
            o = hg_out[cc, hh]
            on = o * lax.rsqrt(jnp.mean(o * o, axis=-1, keepdims=True) + EPS) * hgw_ref[...]
            gh = proj_s[rows, 3 * hw + hh * hd:3 * hw + (hh + 1) * hd]
            mix_s[rows, head_cols[hh]] = _bf(on * _silu(gh))

        qkv_raw = proj_s[:, 4 * hw:7 * hw]
        qkv = _silu(_causal_conv(qkv_raw, cprev_s[...], cw_ref, None))
        cprev_s[...] = qkv_raw[ts - SUBLANES:ts, :]
        ab = proj_s[:, 8 * hw:8 * hw + LANES]
        g_all = -jnp.exp(alog_ref[...]) * _softplus(ab + dtb_ref[...])
        beta_all = jax.nn.sigmoid(ab)
        incl64, strict64 = _block_masks(ts, GD_CHUNK)
        gcum = _dot_exact_rhs(jnp.where(incl64, 1.0, 0.0).astype(BF16), g_all)
        gcum_t = gcum.T
        n_gd = ts // GD_CHUNK
        gd_q, gd_k, gd_gcol, gd_qk, a_mats, rhs_all = [], [], [], [], [], []
        for hh in range(HEADS):
            qh = qkv[:, hh * hd:(hh + 1) * hd]
            kh = qkv[:, hw + hh * hd:hw + (hh + 1) * hd]
            vh = qkv[:, 2 * hw + hh * hd:2 * hw + (hh + 1) * hd]
            qh = qh * lax.rsqrt(jnp.sum(qh * qh, axis=-1, keepdims=True) + EPS) * (hd ** -0.5)
            kh = kh * lax.rsqrt(jnp.sum(kh * kh, axis=-1, keepdims=True) + EPS)
            gcol = gcum[:, hh:hh + 1]
            grow = gcum_t[hh:hh + 1, :]
            beta = beta_all[:, HEADS + hh:HEADS + hh + 1]
            decay = jnp.where(incl64, jnp.exp(jnp.where(incl64, gcol - grow, 0.0)), 0.0)
            kb = kh * beta
            khb = _bf(kh)
            a_mats.append(jnp.where(strict64, _dot_nt(_bf(kb), khb) * decay, 0.0))
            gd_qk.append(_bf(jnp.where(incl64, _dot_nt(_bf(qh), khb) * decay, 0.0)))
            egc = jnp.exp(gcol)
            rhs_all.append(_bf(jnp.concatenate([vh * beta, kb * egc], axis=1)))
            gd_q.append(_bf(qh * egc))
            gd_k.append(kh)
            gd_gcol.append(gcol)
        t_inv = _unit_lower_inverses(a_mats, ts, GD_CHUNK, GD_BASE)
        sol = [_dot(_bf(t), rh) for t, rh in zip(t_inv, rhs_all)]
        chunk_rows = [slice(cc * GD_CHUNK, (cc + 1) * GD_CHUNK) for cc in range(n_gd)]
        gd_states = [[None] * n_gd for _ in range(HEADS)]
        ktu = [[None] * n_gd for _ in range(HEADS)]
        ktw = [[None] * n_gd for _ in range(HEADS)]
        g_last = [[None] * n_gd for _ in range(HEADS)]
        for cc, rows in enumerate(chunk_rows):
            for hh in range(HEADS):
                gcol = gd_gcol[hh]
                g_last[hh][cc] = gcol[(cc + 1) * GD_CHUNK - 1:(cc + 1) * GD_CHUNK, :]
                k_g = _bf(gd_k[hh][rows] * jnp.exp(g_last[hh][cc] - gcol[rows]))
                ktu[hh][cc] = _dot_tn(k_g, _bf(sol[hh][rows, :hd]))
                ktw[hh][cc] = _bf(_dot_tn(k_g, _bf(sol[hh][rows, hd:])))
        st = [gd_state_s[hh] for hh in range(HEADS)]
        for cc in range(n_gd):
            for hh in range(HEADS):
                gd_states[hh][cc] = _bf(st[hh])
                st[hh] = st[hh] * jnp.exp(g_last[hh][cc]) + ktu[hh][cc] - _dot(ktw[hh][cc], gd_states[hh][cc])
        for hh in range(HEADS):
            gd_state_s[hh] = st[hh]
        gd_pairs = [(cc, hh) for cc in range(n_gd) for hh in range(HEADS)]
        wq = {(cc, hh): _dot(jnp.concatenate([_bf(sol[hh][chunk_rows[cc], hd:]), gd_q[hh][chunk_rows[cc]]], axis=0),
                             gd_states[hh][cc]) for cc, hh in gd_pairs}
        v_newb = {(cc, hh): _bf(sol[hh][chunk_rows[cc], :hd] - wq[cc, hh][:GD_CHUNK]) for cc, hh in gd_pairs}
        gd_out = {(cc, hh): wq[cc, hh][GD_CHUNK:] + _dot(gd_qk[hh][chunk_rows[cc], chunk_rows[cc]], v_newb[cc, hh])
                  for cc, hh in gd_pairs}
        for cc, hh in gd_pairs:
            rows = chunk_rows[cc]
            o = gd_out[cc, hh]
            on = o * lax.rsqrt(jnp.mean(o * o, axis=-1, keepdims=True) + EPS) * gdw_ref[...]
            zh = proj_s[rows, 7 * hw + hh * hd:7 * hw + (hh + 1) * hd]
            mix_s[rows, hw + hh * hd:hw + (hh + 1) * hd] = _bf(on * _silu(zh))

    @pl.when(pl.program_id(1) == 0)
    def _():
        cprev_s[...] = jnp.zeros_like(cprev_s)
        hg_state_s[...] = jnp.zeros_like(hg_state_s)
        gd_state_s[...] = jnp.zeros_like(gd_state_s)

    x = x_ref[0]
    shift, scale, gate = mod_ref[0, 0:1, :], mod_ref[0, 1:2, :], mod_ref[0, 2:3, :]
    h = _rmsnorm(x, nw_ref[...]) * (1.0 + scale) + shift
    proj_all[...] = _dot(_bf(h), win_ref[...])
    for t in range(n_sub):
        rows = pl.ds(t * ts, ts)
        mix_tile(proj_all.at[rows], mix_all.at[rows])
    o_ref[0] = x + gate * _dot(mix_all[...], wout_ref[...])


def _odd_mixer(x, mod, norm_w, lb_logits, layer, w_in, hg_norm_w, conv_w, a_log, dt_bias, gd_norm_w, w_out,
               ts=256, n_sub=2):
    bsz, seq, d = x.shape
    hw = HEADS * HEAD_DIM
    n_in = 8 * hw + LANES
    pad = n_in - w_in.shape[1]
    w_in_p = jnp.concatenate([w_in, jnp.zeros((d, pad), w_in.dtype)], axis=1)
    lane_pad = lambda v, off: jnp.zeros((1, LANES), F32).at[0, off:off + v.shape[0]].set(v.astype(F32))
    row = lambda v: v.reshape(1, -1).astype(F32)
    depth = lb_logits.shape[0]
    rows = n_sub * ts
    kern = functools.partial(_odd_kernel, ts=ts, layer=layer)
    return pl.pallas_call(
        kern,
        grid=(bsz, seq // rows),
        in_specs=[pl.BlockSpec((1, rows, d), lambda b, j: (b, j, 0)),
                  pl.BlockSpec((1, 6, d), lambda b, j: (b, 0, 0)),
                  _const_spec((1, d)), _const_spec((d, n_in)), _const_spec((depth, hw)),
                  _const_spec((1, HEAD_DIM)), _const_spec((CONV_WIDTH, 3 * hw)),
                  _const_spec((1, LANES)), _const_spec((1, LANES)), _const_spec((1, HEAD_DIM)),
                  _const_spec((2 * hw, d))],
        out_specs=pl.BlockSpec((1, rows, d), lambda b, j: (b, j, 0)),
        out_shape=jax.ShapeDtypeStruct(x.shape, F32),
        scratch_shapes=[pltpu.VMEM((rows, n_in), F32), pltpu.VMEM((ts, hw), F32),
                        pltpu.VMEM((ts, hw), BF16), pltpu.VMEM((ts, hw), BF16),
                        pltpu.VMEM((rows, 2 * hw), BF16), pltpu.VMEM((SUBLANES, 3 * hw), F32),
                        pltpu.VMEM((HEADS, HEAD_DIM, HEAD_DIM), F32), pltpu.VMEM((HEADS, HEAD_DIM, HEAD_DIM), F32)],
        compiler_params=pltpu.CompilerParams(
            dimension_semantics=("arbitrary", "arbitrary"), vmem_limit_bytes=VMEM_LIMIT_BYTES),
        name="odd_mixer",
    )(x, mod, row(norm_w), _bf(w_in_p), lb_logits.astype(F32), row(hg_norm_w), conv_w.astype(F32),
      lane_pad(a_log, 0), lane_pad(dt_bias, 0), row(gd_norm_w), _bf(w_out))


def _mlp_kernel(x_ref, mod_ref, nw_ref, w1_ref, w2_ref, fw_ref, o_ref, *, n_split, final):
    x = x_ref[0]
    shift, scale, gate = mod_ref[0, 3:4, :], mod_ref[0, 4:5, :], mod_ref[0, 5:6, :]
    h = _bf(_rmsnorm(x, nw_ref[...]) * (1.0 + scale) + shift)
    hidden = w1_ref.shape[1]
    step = hidden // n_split
    acc = jnp.zeros(x.shape, F32)
    for s in range(n_split):
        hid = jnp.square(jnp.maximum(_dot(h, w1_ref[:, s * step:(s + 1) * step]), 0.0))
        acc = acc + _dot(_bf(hid), w2_ref[s * step:(s + 1) * step, :])
    y = x + gate * acc
    if final:
        y = _rmsnorm(y, fw_ref[...])
    o_ref[0] = y


def _mlp(x, mod, norm_w, w1, w2, final_w, final, tm=512, n_split=4):
    bsz, seq, d = x.shape
    hidden = w1.shape[1]
    kern = functools.partial(_mlp_kernel, n_split=n_split, final=final)
    row = lambda v: v.reshape(1, -1).astype(F32)
    return pl.pallas_call(
        kern,
        grid=(bsz, seq // tm),
        in_specs=[pl.BlockSpec((1, tm, d), lambda b, j: (b, j, 0)),
                  pl.BlockSpec((1, 6, d), lambda b, j: (b, 0, 0)),
                  _const_spec((1, d)), _const_spec((d, hidden)), _const_spec((hidden, d)), _const_spec((1, d))],
        out_specs=pl.BlockSpec((1, tm, d), lambda b, j: (b, j, 0)),
        out_shape=jax.ShapeDtypeStruct(x.shape, F32),
        compiler_params=pltpu.CompilerParams(
            dimension_semantics=("arbitrary", "arbitrary"), vmem_limit_bytes=VMEM_LIMIT_BYTES),
        name="mlp_final" if final else "mlp",
    )(x, mod, row(norm_w), _bf(w1), _bf(w2), row(final_w))


def kernel(x, c, positions, ada_w, ada_b, norm_mix_w, norm_mlp_w, mlp_w1, mlp_w2, final_norm_w, ev_w_in, lru_conv_w, lru_conv_b, lru_w_a, lru_b_a, lru_w_x, lru_b_x, lru_lambda, ev_w_out, hg_lb_logits, od_w_in, hg_norm_w, gd_conv_w, gd_a_log, gd_dt_bias, gd_norm_w, od_w_out):
    depth = ada_w.shape[0]
    bsz, seq, d = x.shape
    mod = _adaln_mod(c, ada_w, ada_b).reshape(depth, bsz, 6, d)
    posf = positions.astype(F32)[..., None]
    for layer in range(depth):
        j = layer // 2
        if layer % 2 == 0:
            x = _even_mixer(x, mod[layer], posf, norm_mix_w[layer], ev_w_in[j], lru_conv_w[j], lru_conv_b[j],
                            lru_w_a[j], lru_b_a[j], lru_w_x[j], lru_b_x[j], lru_lambda[j], ev_w_out[j])
        else:
            x = _odd_mixer(x, mod[layer], norm_mix_w[layer], hg_lb_logits, layer, od_w_in[j], hg_norm_w[j],
                           gd_conv_w[j], gd_a_log[j], gd_dt_bias[j], gd_norm_w[j], od_w_out[j])
        x = _mlp(x, mod[layer], norm_mlp_w[layer], mlp_w1[layer], mlp_w2[layer], final_norm_w,
                 final=(layer == depth - 1))
    return x
```

```python
import functools
import math

import jax
import jax.numpy as jnp
from jax import lax
from jax.experimental import pallas as pl
from jax.experimental.pallas import tpu as pltpu

F32 = jnp.float32
BF16 = jnp.bfloat16

EPS = 1e-6
CONV_WIDTH = 4
LRU_C = 8.0
ROPE_BASE = 10000.0
HEADS = 4
HEAD_DIM = 128
HG_CHUNK = 32
GD_CHUNK = 64
GD_BASE = 16
SUBLANES = 8
LANES = 128
VMEM_LIMIT_BYTES = 56 * 1024 * 1024


def _dot(a, b):
    return jnp.dot(a, b, preferred_element_type=F32)


def _dot_nt(a, b):
    return lax.dot_general(a, b, (((1,), (1,)), ((), ())), preferred_element_type=F32)


def _dot_tn(a, b):
    return lax.dot_general(a, b, (((0,), (0,)), ((), ())), preferred_element_type=F32)


def _bf(x):
    return x.astype(BF16)


def _dot_exact_rhs(sel_bf16, x):
    hi = _bf(x)
    r1 = x - hi.astype(F32)
    mid = _bf(r1)
    lo = _bf(r1 - mid.astype(F32))
    return _dot(sel_bf16, hi) + _dot(sel_bf16, mid) + _dot(sel_bf16, lo)


def _softplus(x):
    return jnp.maximum(x, 0.0) + jnp.log1p(jnp.exp(-jnp.abs(x)))


def _neg_expm1(x):
    t = jnp.tanh(0.5 * x)
    return -2.0 * t / (1.0 - t)


def _silu(x):
    return x * jax.nn.sigmoid(x)


def _rmsnorm(x, w):
    return x * lax.rsqrt(jnp.mean(x * x, axis=-1, keepdims=True) + EPS) * w


def _norm_modulate(x, w, shift, scale):
    return x * lax.rsqrt(jnp.mean(x * x, axis=-1, keepdims=True) + EPS) * (w * (1.0 + scale)) + shift


def _shift_rows(x, prev8, s):
    rolled = pltpu.roll(x, s, axis=0)
    row = lax.broadcasted_iota(jnp.int32, (SUBLANES, x.shape[1]), 0)
    top = jnp.where(row < s, pltpu.roll(prev8, s, axis=0), rolled[0:SUBLANES])
    return jnp.concatenate([top, rolled[SUBLANES:]], axis=0)


def _causal_conv(x, prev8, w_ref, bias):
    acc = x * w_ref[CONV_WIDTH - 1:CONV_WIDTH, :]
    if bias is not None:
        acc = acc + bias
    for s in range(1, CONV_WIDTH):
        acc = acc + _shift_rows(x, prev8, s) * w_ref[CONV_WIDTH - 1 - s:CONV_WIDTH - s, :]
    return acc


def _same_block(n, blk):
    r = lax.broadcasted_iota(jnp.int32, (n, n), 0)
    c = lax.broadcasted_iota(jnp.int32, (n, n), 1)
    sh = int(math.log2(blk))
    return lax.shift_right_logical(r, sh) == lax.shift_right_logical(c, sh)


def _block_masks(n, blk):
    r = lax.broadcasted_iota(jnp.int32, (n, n), 0)
    c = lax.broadcasted_iota(jnp.int32, (n, n), 1)
    same = _same_block(n, blk)
    return same & (c <= r), same & (c < r)


def _unit_lower_inverses(a_mats, n, blk, base):
    r = lax.broadcasted_iota(jnp.int32, (n, n), 0)
    c = lax.broadcasted_iota(jnp.int32, (n, n), 1)
    inner = _same_block(n, base)
    pw = [jnp.where(inner, a, 0.0) for a in a_mats]
    t_inv = [jnp.where(r == c, 1.0, 0.0) - p for p in pw]
    for _ in range(int(math.log2(base)) - 1):
        pwb = [_bf(p) for p in pw]
        pw = [_dot(p, p) for p in pwb]
        t_inv = [t + _dot(_bf(t), _bf(p)) for t, p in zip(t_inv, pw)]
    size = base
    while size < blk:
        size *= 2
        outer = _same_block(n, size)
        off = [_bf(jnp.where(outer & jnp.logical_not(inner), a, 0.0)) for a in a_mats]
        tb = [_bf(t) for t in t_inv]
        t_inv = [t - _dot(b, _bf(_dot(o, b))) for t, b, o in zip(t_inv, tb, off)]
        inner = outer
    return t_inv


def _mod_kernel(c_ref, w_ref, b_ref, o_ref):
    c = c_ref[...]
    o_ref[0] = _dot(_bf(_silu(c)), _bf(w_ref[0])) + b_ref[0]


def _adaln_mod(c, ada_w, ada_b, tn=1536):
    depth, d, n = ada_w.shape
    b = c.shape[0]
    return pl.pallas_call(
        _mod_kernel,
        grid=(depth, n // tn),
        in_specs=[pl.BlockSpec((b, d), lambda l, j: (0, 0)),
                  pl.BlockSpec((1, d, tn), lambda l, j: (l, 0, j)),
                  pl.BlockSpec((1, 1, tn), lambda l, j: (l, 0, j))],
        out_specs=pl.BlockSpec((1, b, tn), lambda l, j: (l, 0, j)),
        out_shape=jax.ShapeDtypeStruct((depth, b, n), F32),
        compiler_params=pltpu.CompilerParams(
            dimension_semantics=("arbitrary", "arbitrary"), vmem_limit_bytes=VMEM_LIMIT_BYTES),
        name="adaln_mod",
    )(c, ada_w, ada_b.reshape(depth, 1, n))


def _lru_scan(a_s, u_s, h0, ts):
    width = a_s.shape[1]
    row = lax.broadcasted_iota(jnp.int32, (SUBLANES, width), 0)

    hprev = h0
    for g in range(ts // SUBLANES):
        rows = slice(g * SUBLANES, (g + 1) * SUBLANES)
        a8 = a_s[rows, :]
        u8 = u_s[rows, :]
        for d in (1, 2, 4):
            keep = row >= d
            a_sh = jnp.where(keep, pltpu.roll(a8, d, axis=0), 1.0)
            u_sh = jnp.where(keep, pltpu.roll(u8, d, axis=0), 0.0)
            u8 = u8 + a8 * u_sh
            a8 = a8 * a_sh
        h8 = u8 + a8 * hprev
        u_s[rows, :] = h8
        hprev = h8[SUBLANES - 1:SUBLANES, :]
    return hprev


def _even_kernel(x_ref, mod_ref, pos_ref, off_ref, coff_ref, soff_ref, inv_ref, sgn_ref, nw_ref, win_ref, cw_ref,
                 cb_ref, wga_ref, wgx_ref, ba_ref, bx_ref, lam_ref, wout_ref, o_ref,
                 proj_all, a_s, u_s, mix_all, cos_s, sin_s, xprev_s, hprev_s, state_s, *, ts, chunk):
    lw = a_s.shape[1]
    half = lw // 2
    hd = HEAD_DIM
    n_sub = proj_all.shape[0] // ts

    def mix_tile(proj_s, mix_s, cos_view, sin_view):
        xr = proj_s[:, 0:lw]
        xc = _causal_conv(xr, xprev_s[...], cw_ref, cb_ref[...])
        xprev_s[...] = xr[ts - SUBLANES:ts, :]
        xcb = _bf(xc)
        r_pre = jnp.concatenate([_dot(xcb[:, :half], wga_ref[0]), _dot(xcb[:, half:], wga_ref[1])], axis=1)
        i_pre = jnp.concatenate([_dot(xcb[:, :half], wgx_ref[0]), _dot(xcb[:, half:], wgx_ref[1])], axis=1)
        r = jax.nn.sigmoid(r_pre + ba_ref[...])
        i = jax.nn.sigmoid(i_pre + bx_ref[...])
        log_a = (-LRU_C) * r * _softplus(-lam_ref[...])
        a_s[...] = jnp.exp(log_a)
        u_s[...] = jnp.sqrt(_neg_expm1(2.0 * log_a)) * (i * xc)
        hprev_s[...] = _lru_scan(a_s, u_s, hprev_s[...], ts)
        yr = proj_s[:, lw:2 * lw]
        mix_s[:, 0:lw] = _bf(u_s[...] * jax.nn.gelu(yr, approximate=True))

        cos_p, sin_p = cos_view[...], sin_view[...]
        lane = lax.broadcasted_iota(jnp.int32, (ts // 2, hd), 1)
        first = lane < hd // 2
        cos_r, sin_r = pltpu.roll(cos_p, hd // 2, axis=1), pltpu.roll(sin_p, hd // 2, axis=1)
        cos_d = jnp.concatenate([jnp.where(first, cos_p, cos_r), jnp.where(first, cos_r, cos_p)], axis=0)
        sin_d = jnp.concatenate([jnp.where(first, sin_p, sin_r), jnp.where(first, sin_r, sin_p)], axis=0)
        sin_d = sin_d * sgn_ref[...]
        ridx = lax.broadcasted_iota(jnp.int32, (chunk, chunk), 0)
        cidx = lax.broadcasted_iota(jnp.int32, (chunk, chunk), 1)
        rel = (ridx - cidx).astype(F32)
        tcol = lax.broadcasted_iota(jnp.int32, (chunk, 1), 0).astype(F32)
        qoff, koff, voff, goff = 2 * lw, 2 * lw + HEADS * hd, 2 * lw + 2 * HEADS * hd, 2 * lw + 3 * HEADS * hd
        log_gamma = [math.log1p(-(2.0 ** (-5.0 - hh))) for hh in range(HEADS)]
        dmats = [jnp.where(rel >= 0, jnp.exp(lg * jnp.maximum(rel, 0.0)), 0.0) for lg in log_gamma]
        q_decs = [jnp.exp(lg * (tcol + 1.0)) for lg in log_gamma]
        k_decs = [jnp.exp(lg * (chunk - 1.0 - tcol)) for lg in log_gamma]
        n_ret = ts // chunk
        pairs = [(cc, hh) for cc in range(n_ret) for hh in range(HEADS)]
        qb, vb, sc, kv = {}, {}, {}, {}
        for cc, hh in pairs:
            rows = slice(cc * chunk, (cc + 1) * chunk)
            cos_c, sin_c = cos_d[rows], sin_d[rows]
            qc = proj_s[rows, qoff + hh * hd:qoff + (hh + 1) * hd]
            kc = proj_s[rows, koff + hh * hd:koff + (hh + 1) * hd]
            qc = qc * cos_c + pltpu.roll(qc, hd // 2, axis=1) * sin_c
            kc = (kc * cos_c + pltpu.roll(kc, hd // 2, axis=1) * sin_c) * (hd ** -0.5)
            vb[cc, hh] = _bf(proj_s[rows, voff + hh * hd:voff + (hh + 1) * hd])
            qb[cc, hh] = _bf(qc)
            sc[cc, hh] = _bf(_dot_nt(qb[cc, hh], _bf(kc)) * dmats[hh])
            kv[cc, hh] = _dot_tn(_bf(kc * k_decs[hh]), vb[cc, hh])
        st_in = {}
        for hh, lg in enumerate(log_gamma):
            st = state_s[hh]
            for cc in range(n_ret):
                st_in[cc, hh] = _bf(st)
                st = math.exp(lg * chunk) * st + kv[cc, hh]
            state_s[hh] = st
        outs = {p: _dot(sc[p], vb[p]) + _dot(qb[p], st_in[p]) * q_decs[p[1]] for p in pairs}
        for cc, hh in pairs:
            rows = slice(cc * chunk, (cc + 1) * chunk)
            o = outs[cc, hh]
            mu = jnp.mean(o, axis=-1, keepdims=True)
            oc = o - mu
            var = jnp.mean(oc * oc, axis=-1, keepdims=True)
            g = proj_s[rows, goff + hh * hd:goff + (hh + 1) * hd]
            mix_s[rows, lw + hh * hd:lw + (hh + 1) * hd] = _bf(oc * lax.rsqrt(var + EPS) * _silu(g))

    @pl.when(pl.program_id(1) == 0)
    def _():
        xprev_s[...] = jnp.zeros_like(xprev_s)
        hprev_s[...] = jnp.zeros_like(hprev_s)
        state_s[...] = jnp.zeros_like(state_s)

    pos = pos_ref[0]
    base = pos[0:1, 0:1]
    dev = jnp.abs(pos - base - off_ref[...])
    dev = jnp.max(jnp.max(dev, axis=1, keepdims=True), axis=0, keepdims=True)
    consecutive = dev[0, 0] == 0.0

    @pl.when(consecutive)
    def _():
        ang0 = base * inv_ref[...]
        c0, s0 = jnp.cos(ang0), jnp.sin(ang0)
        cos_s[...] = c0 * coff_ref[...] - s0 * soff_ref[...]
        sin_s[...] = s0 * coff_ref[...] + c0 * soff_ref[...]

    @pl.when(jnp.logical_not(consecutive))
    def _():
        ang = pos * inv_ref[...]
        cos_s[...] = jnp.cos(ang)
        sin_s[...] = jnp.sin(ang)

    x = x_ref[0]
    shift, scale, gate = mod_ref[0, 0:1, :], mod_ref[0, 1:2, :], mod_ref[0, 2:3, :]
    proj_all[...] = _dot(_bf(_norm_modulate(x, nw_ref[...], shift, scale)), win_ref[...])
    for t in range(n_sub):
        rows = pl.ds(t * ts, ts)
        packed = pl.ds(t * (ts // 2), ts // 2)
        mix_tile(proj_all.at[rows], mix_all.at[rows], cos_s.at[packed], sin_s.at[packed])
    o_ref[0] = x + gate * _dot(mix_all[...], wout_ref[...])


def _blockdiag_halves(w):
    n, bi, bj = w.shape
    eye = jnp.eye(n // 2, dtype=w.dtype)
    halves = []
    for p in range(2):
        blocks = w[p * (n // 2):(p + 1) * (n // 2)]
        dense = jnp.einsum('nm,nij->nimj', eye, blocks).reshape(n // 2 * bi, n // 2 * bj)
        halves.append(dense)
    return jnp.stack(halves)


def _const_spec(shape):
    nd = len(shape)
    return pl.BlockSpec(shape, lambda *_: (0,) * nd, pipeline_mode=pl.Buffered(1))


def _pack_half_tiles(v, ts):
    lead = v.shape[:-1]
    p = v.reshape(lead + (v.shape[-1] // ts, 2, ts // 2))
    p = jnp.swapaxes(p, -1, -2)
    p = jnp.repeat(p, HEAD_DIM // 2, axis=-1)
    return p.reshape(lead + (v.shape[-1] // 2, HEAD_DIM))


def _even_mixer(x, mod, positions, norm_w, w_in, conv_w, conv_b, w_a, b_a, w_x, b_x, lam, w_out,
                ts=256, n_sub=4, chunk=128):
    bsz, seq, d = x.shape
    lw = lam.shape[-1]
    n_in = w_in.shape[1]
    half = HEAD_DIM // 2
    inv = ROPE_BASE ** (-jnp.arange(half, dtype=F32) / half)
    inv2 = jnp.concatenate([inv, inv]).reshape(1, HEAD_DIM)
    sgn = jnp.concatenate([-jnp.ones((half,), F32), jnp.ones((half,), F32)]).reshape(1, HEAD_DIM)
    row = lambda v: v.reshape(1, -1).astype(F32)
    rows = n_sub * ts
    pos_packed = _pack_half_tiles(positions.astype(F32), ts)
    off_packed = _pack_half_tiles(jnp.arange(rows, dtype=F32), ts)
    cos_off, sin_off = jnp.cos(off_packed * inv2), jnp.sin(off_packed * inv2)
    kern = functools.partial(_even_kernel, ts=ts, chunk=chunk)
    return pl.pallas_call(
        kern,
        grid=(bsz, seq // rows),
        in_specs=[pl.BlockSpec((1, rows, d), lambda b, j: (b, j, 0)),
                  pl.BlockSpec((1, 6, d), lambda b, j: (b, 0, 0)),
                  pl.BlockSpec((1, rows // 2, HEAD_DIM), lambda b, j: (b, j, 0)),
                  _const_spec((rows // 2, HEAD_DIM)), _const_spec((rows // 2, HEAD_DIM)),
                  _const_spec((rows // 2, HEAD_DIM)),
                  _const_spec((1, HEAD_DIM)), _const_spec((1, HEAD_DIM)), _const_spec((1, d)),
                  _const_spec((d, n_in)), _const_spec((CONV_WIDTH, lw)), _const_spec((1, lw)),
                  _const_spec((2, lw // 2, lw // 2)), _const_spec((2, lw // 2, lw // 2)),
                  _const_spec((1, lw)), _const_spec((1, lw)), _const_spec((1, lw)),
                  _const_spec((lw + HEADS * HEAD_DIM, d))],
        out_specs=pl.BlockSpec((1, rows, d), lambda b, j: (b, j, 0)),
        out_shape=jax.ShapeDtypeStruct(x.shape, F32),
        scratch_shapes=[pltpu.VMEM((rows, n_in), F32), pltpu.VMEM((ts, lw), F32), pltpu.VMEM((ts, lw), F32),
                        pltpu.VMEM((rows, lw + HEADS * HEAD_DIM), BF16),
                        pltpu.VMEM((rows // 2, HEAD_DIM), F32), pltpu.VMEM((rows // 2, HEAD_DIM), F32),
                        pltpu.VMEM((SUBLANES, lw), F32),
                        pltpu.VMEM((1, lw), F32), pltpu.VMEM((HEADS, HEAD_DIM, HEAD_DIM), F32)],
        compiler_params=pltpu.CompilerParams(
            dimension_semantics=("arbitrary", "arbitrary"), vmem_limit_bytes=VMEM_LIMIT_BYTES),
        name="even_mixer",
    )(x, mod, pos_packed, off_packed, cos_off, sin_off, inv2, sgn, row(norm_w), _bf(w_in), conv_w.astype(F32), row(conv_b),
      _bf(_blockdiag_halves(w_a)), _bf(_blockdiag_halves(w_x)), row(b_a), row(b_x), row(lam), _bf(w_out))


def _odd_kernel(x_ref, mod_ref, nw_ref, win_ref, lbl_ref, hgw_ref, cw_ref, alog_ref, dtb_ref, gdw_ref,
                wout_ref, o_ref, proj_all, b_s, qi_s, ki_s, mix_all, cprev_s, hg_state_s, gd_state_s,
                *, ts, layer):
    hd = HEAD_DIM
    hw = HEADS * hd
    n_sub = proj_all.shape[0] // ts

    def mix_tile(proj_s, mix_s):
        logits = lbl_ref[...]
        e = jnp.exp(logits - jnp.max(logits, axis=0, keepdims=True))
        lb = jnp.sum(e[1:layer + 1], axis=0, keepdims=True) / jnp.sum(e, axis=0, keepdims=True)
        key = (1.0 - lb) * jax.nn.sigmoid(-proj_s[:, hw:2 * hw])
        log_f = jnp.log1p(-key)
        incl32, _ = _block_masks(ts, HG_CHUNK)
        b_s[...] = _dot_exact_rhs(jnp.where(incl32, 1.0, 0.0).astype(BF16), log_f)
        mid = HG_CHUNK // 2
        n_hg = ts // HG_CHUNK
        for cc in range(n_hg):
            rows = slice(cc * HG_CHUNK, (cc + 1) * HG_CHUNK)
            b = b_s[rows, :]
            b_mid = b[mid:mid + 1, :]
            qi_s[rows, :] = _bf(proj_s[rows, 0:hw] * jnp.exp(b - b_mid))
            ki_s[rows, :] = _bf(key[rows] * jnp.exp(b_mid - b))
        head_cols = [slice(hh * hd, (hh + 1) * hd) for hh in range(HEADS)]
        hg_v = [_bf(proj_s[:, 2 * hw + hh * hd:2 * hw + (hh + 1) * hd]) for hh in range(HEADS)]
        hg_scores = [jnp.where(incl32, _dot_nt(qi_s[:, cols], ki_s[:, cols]), 0.0) for cols in head_cols]
        hg_intra = [_dot(_bf(sc), v) for sc, v in zip(hg_scores, hg_v)]
        hg_pairs = [(cc, hh) for cc in range(n_hg) for hh in range(HEADS)]
        hg_rows = [slice(cc * HG_CHUNK, (cc + 1) * HG_CHUNK) for cc in range(n_hg)]
        hg_kv, hg_dec, hg_qs = {}, {}, {}
        for cc, hh in hg_pairs:
            rows, cols = hg_rows[cc], head_cols[hh]
            b = b_s[rows, cols]
            b_last = b[HG_CHUNK - 1:HG_CHUNK, :]
            hg_dec[cc, hh] = jnp.exp(b_last)
            hg_qs[cc, hh] = _bf(proj_s[rows, cols] * jnp.exp(b))
            hg_kv[cc, hh] = _dot_tn(hg_v[hh][rows], _bf(key[rows, cols] * jnp.exp(b_last - b)))
        hg_in = {}
        for hh in range(HEADS):
            st_t = hg_state_s[hh]
            for cc in range(n_hg):
                hg_in[cc, hh] = _bf(st_t)
                st_t = st_t * hg_dec[cc, hh] + hg_kv[cc, hh]
            hg_state_s[hh] = st_t
        hg_out = {p: hg_intra[p[1]][hg_rows[p[0]]] + _dot_nt(hg_qs[p], hg_in[p]) for p in hg_pairs}
        for cc, hh in hg_pairs:
            rows = hg_rows[cc]
            o = hg_out[cc, hh]
            on = o * lax.rsqrt(jnp.mean(o * o, axis=-1, keepdims=True) + EPS) * hgw_ref[...]
            gh = proj_s[rows, 3 * hw + hh * hd:3 * hw + (hh + 1) * hd]
            mix_s[rows, head_cols[hh]] = _bf(on * _silu(gh))

        qkv_raw = proj_s[:, 4 * hw:7 * hw]
        qkv = _silu(_causal_conv(qkv_raw, cprev_s[...], cw_ref, None))
        cprev_s[...] = qkv_raw[ts - SUBLANES:ts, :]
        ab = proj_s[:, 8 * hw:8 * hw + LANES]
        g_all = -jnp.exp(alog_ref[...]) * _softplus(ab + dtb_ref[...])
        beta_all = jax.nn.sigmoid(ab)
        incl64, strict64 = _block_masks(ts, GD_CHUNK)
        gcum = _dot_exact_rhs(jnp.where(incl64, 1.0, 0.0).astype(BF16), g_all)
        gcum_t = gcum.T
        n_gd = ts // GD_CHUNK
        gd_q, gd_k, gd_gcol, gd_qk, a_mats, rhs_all = [], [], [], [], [], []
        for hh in range(HEADS):
            qh = qkv[:, hh * hd:(hh + 1) * hd]
            kh = qkv[:, hw + hh * hd:hw + (hh + 1) * hd]
            vh = qkv[:, 2 * hw + hh * hd:2 * hw + (hh + 1) * hd]
            qh = qh * lax.rsqrt(jnp.sum(qh * qh, axis=-1, keepdims=True) + EPS) * (hd ** -0.5)
            kh = kh * lax.rsqrt(jnp.sum(kh * kh, axis=-1, keepdims=True) + EPS)
            gcol = gcum[:, hh:hh + 1]
            grow = gcum_t[hh:hh + 1, :]
            beta = beta_all[:, HEADS + hh:HEADS + hh + 1]
            decay = jnp.where(incl64, jnp.exp(jnp.where(incl64, gcol - grow, 0.0)), 0.0)
            kb = kh * beta
            khb = _bf(kh)
            a_mats.append(jnp.where(strict64, _dot_nt(_bf(kb), khb) * decay, 0.0))
            gd_qk.append(_bf(jnp.where(incl64, _dot_nt(_bf(qh), khb) * decay, 0.0)))
            egc = jnp.exp(gcol)
            rhs_all.append(_bf(jnp.concatenate([vh * beta, kb * egc], axis=1)))
            gd_q.append(_bf(qh * egc))
            gd_k.append(kh)
            gd_gcol.append(gcol)
        t_inv = _unit_lower_inverses(a_mats, ts, GD_CHUNK, GD_BASE)
        sol = [_dot(_bf(t), rh) for t, rh in zip(t_inv, rhs_all)]
        chunk_rows = [slice(cc * GD_CHUNK, (cc + 1) * GD_CHUNK) for cc in range(n_gd)]
        gd_states = [[None] * n_gd for _ in range(HEADS)]
        ktu = [[None] * n_gd for _ in range(HEADS)]
        ktw = [[None] * n_gd for _ in range(HEADS)]
        g_last = [[None] * n_gd for _ in range(HEADS)]
        for cc, rows in enumerate(chunk_rows):
            for hh in range(HEADS):
                gcol = gd_gcol[hh]
                g_last[hh][cc] = gcol[(cc + 1) * GD_CHUNK - 1:(cc + 1) * GD_CHUNK, :]
                k_g = _bf(gd_k[hh][rows] * jnp.exp(g_last[hh][cc] - gcol[rows]))
                ktu[hh][cc] = _dot_tn(k_g, _bf(sol[hh][rows, :hd]))
                ktw[hh][cc] = _bf(_dot_tn(k_g, _bf(sol[hh][rows, hd:])))
        st = [gd_state_s[hh] for hh in range(HEADS)]
        for cc in range(n_gd):
            for hh in range(HEADS):
                gd_states[hh][cc] = _bf(st[hh])
                st[hh] = st[hh] * jnp.exp(g_last[hh][cc]) + ktu[hh][cc] - _dot(ktw[hh][cc], gd_states[hh][cc])
        for hh in range(HEADS):
            gd_state_s[hh] = st[hh]
        gd_pairs = [(cc, hh) for cc in range(n_gd) for hh in range(HEADS)]
        wq = {(cc, hh): _dot(jnp.concatenate([_bf(sol[hh][chunk_rows[cc], hd:]), gd_q[hh][chunk_rows[cc]]], axis=0),
                             gd_states[hh][cc]) for cc, hh in gd_pairs}
        v_newb = {(cc, hh): _bf(sol[hh][chunk_rows[cc], :hd] - wq[cc, hh][:GD_CHUNK]) for cc, hh in gd_pairs}
        gd_out = {(cc, hh): wq[cc, hh][GD_CHUNK:] + _dot(gd_qk[hh][chunk_rows[cc], chunk_rows[cc]], v_newb[cc, hh])
                  for cc, hh in gd_pairs}
        for cc, hh in gd_pairs:
            rows = chunk_rows[cc]
            o = gd_out[cc, hh]
            on = o * lax.rsqrt(jnp.mean(o * o, axis=-1, keepdims=True) + EPS) * gdw_ref[...]
            zh = proj_s[rows, 7 * hw + hh * hd:7 * hw + (hh + 1) * hd]
            mix_s[rows, hw + hh * hd:hw + (hh + 1) * hd] = _bf(on * _silu(zh))

    @pl.when(pl.program_id(1) == 0)
    def _():
        cprev_s[...] = jnp.zeros_like(cprev_s)
        hg_state_s[...] = jnp.zeros_like(hg_state_s)
        gd_state_s[...] = jnp.zeros_like(gd_state_s)

    x = x_ref[0]
    shift, scale, gate = mod_ref[0, 0:1, :], mod_ref[0, 1:2, :], mod_ref[0, 2:3, :]
    proj_all[...] = _dot(_bf(_norm_modulate(x, nw_ref[...], shift, scale)), win_ref[...])
    for t in range(n_sub):
        rows = pl.ds(t * ts, ts)
        mix_tile(proj_all.at[rows], mix_all.at[rows])
    o_ref[0] = x + gate * _dot(mix_all[...], wout_ref[...])


def _odd_mixer(x, mod, norm_w, lb_logits, layer, w_in, hg_norm_w, conv_w, a_log, dt_bias, gd_norm_w, w_out,
               ts=256, n_sub=2):
    bsz, seq, d = x.shape
    hw = HEADS * HEAD_DIM
    n_in = 8 * hw + LANES
    pad = n_in - w_in.shape[1]
    w_in_p = jnp.concatenate([w_in, jnp.zeros((d, pad), w_in.dtype)], axis=1)
    lane_pad = lambda v, off: jnp.zeros((1, LANES), F32).at[0, off:off + v.shape[0]].set(v.astype(F32))
    row = lambda v: v.reshape(1, -1).astype(F32)
    depth = lb_logits.shape[0]
    rows = n_sub * ts
    kern = functools.partial(_odd_kernel, ts=ts, layer=layer)
    return pl.pallas_call(
        kern,
        grid=(bsz, seq // rows),
        in_specs=[pl.BlockSpec((1, rows, d), lambda b, j: (b, j, 0)),
                  pl.BlockSpec((1, 6, d), lambda b, j: (b, 0, 0)),
                  _const_spec((1, d)), _const_spec((d, n_in)), _const_spec((depth, hw)),
                  _const_spec((1, HEAD_DIM)), _const_spec((CONV_WIDTH, 3 * hw)),
                  _const_spec((1, LANES)), _const_spec((1, LANES)), _const_spec((1, HEAD_DIM)),
                  _const_spec((2 * hw, d))],
        out_specs=pl.BlockSpec((1, rows, d), lambda b, j: (b, j, 0)),
        out_shape=jax.ShapeDtypeStruct(x.shape, F32),
        scratch_shapes=[pltpu.VMEM((rows, n_in), F32), pltpu.VMEM((ts, hw), F32),
                        pltpu.VMEM((ts, hw), BF16), pltpu.VMEM((ts, hw), BF16),
                        pltpu.VMEM((rows, 2 * hw), BF16), pltpu.VMEM((SUBLANES, 3 * hw), F32),
                        pltpu.VMEM((HEADS, HEAD_DIM, HEAD_DIM), F32), pltpu.VMEM((HEADS, HEAD_DIM, HEAD_DIM), F32)],
        compiler_params=pltpu.CompilerParams(
            dimension_semantics=("arbitrary", "arbitrary"), vmem_limit_bytes=VMEM_LIMIT_BYTES),
        name="odd_mixer",
    )(x, mod, row(norm_w), _bf(w_in_p), lb_logits.astype(F32), row(hg_norm_w), conv_w.astype(F32),
      lane_pad(a_log, 0), lane_pad(dt_bias, 0), row(gd_norm_w), _bf(w_out))


def _mlp_kernel(x_ref, mod_ref, nw_ref, w1_ref, w2_ref, fw_ref, o_ref, *, n_split, final):
    x = x_ref[0]
    shift, scale, gate = mod_ref[0, 3:4, :], mod_ref[0, 4:5, :], mod_ref[0, 5:6, :]
    h = _bf(_norm_modulate(x, nw_ref[...], shift, scale))
    hidden = w1_ref.shape[2]
    step = hidden // n_split
    acc = jnp.zeros(x.shape, F32)
    for s in range(n_split):
        hid = jnp.square(jnp.maximum(_dot(h, w1_ref[0, :, s * step:(s + 1) * step]), 0.0))
        acc = acc + _dot(_bf(hid), w2_ref[0, s * step:(s + 1) * step, :])
    y = x + gate * acc
    if final:
        y = _rmsnorm(y, fw_ref[...])
    o_ref[0] = y


def _mlp(x, mod, norm_w, w1_all, w2_all, layer, final_w, final, tm=512, n_split=4):
    bsz, seq, d = x.shape
    hidden = w1_all.shape[2]
    kern = functools.partial(_mlp_kernel, n_split=n_split, final=final)
    row = lambda v: v.reshape(1, -1).astype(F32)
    layer_block = lambda shape: pl.BlockSpec(shape, lambda b, j: (layer, 0, 0), pipeline_mode=pl.Buffered(1))
    return pl.pallas_call(
        kern,
        grid=(bsz, seq // tm),
        in_specs=[pl.BlockSpec((1, tm, d), lambda b, j: (b, j, 0)),
                  pl.BlockSpec((1, 6, d), lambda b, j: (b, 0, 0)),
                  _const_spec((1, d)), layer_block((1, d, hidden)), layer_block((1, hidden, d)),
                  _const_spec((1, d))],
        out_specs=pl.BlockSpec((1, tm, d), lambda b, j: (b, j, 0)),
        out_shape=jax.ShapeDtypeStruct(x.shape, F32),
        compiler_params=pltpu.CompilerParams(
            dimension_semantics=("arbitrary", "arbitrary"), vmem_limit_bytes=VMEM_LIMIT_BYTES),
        name="mlp_final" if final else "mlp",
    )(x, mod, row(norm_w), w1_all, w2_all, row(final_w))


def kernel(x, c, positions, ada_w, ada_b, norm_mix_w, norm_mlp_w, mlp_w1, mlp_w2, final_norm_w, ev_w_in, lru_conv_w, lru_conv_b, lru_w_a, lru_b_a, lru_w_x, lru_b_x, lru_lambda, ev_w_out, hg_lb_logits, od_w_in, hg_norm_w, gd_conv_w, gd_a_log, gd_dt_bias, gd_norm_w, od_w_out):
    depth = ada_w.shape[0]
    bsz, seq, d = x.shape
    mod = _adaln_mod(c, ada_w, ada_b).reshape(depth, bsz, 6, d)
    w1_all, w2_all = _bf(mlp_w1), _bf(mlp_w2)
    for layer in range(depth):
        j = layer // 2
        if layer % 2 == 0:
            x = _even_mixer(x, mod[layer], positions, norm_mix_w[layer], ev_w_in[j], lru_conv_w[j], lru_conv_b[j],
                            lru_w_a[j], lru_b_a[j], lru_w_x[j], lru_b_x[j], lru_lambda[j], ev_w_out[j])
        else:
            x = _odd_mixer(x, mod[layer], norm_mix_w[layer], hg_lb_logits, layer, od_w_in[j], hg_norm_w[j],
                           gd_conv_w[j], gd_a_log[j], gd_dt_bias[j], gd_norm_w[j], od_w_out[j])
        x = _mlp(x, mod[layer], norm_mlp_w[layer], w1_all, w2_all, layer, final_norm_w,
                 final=(layer == depth - 1))
    return x
```

```python
import functools
import math

import jax
import jax.numpy as jnp
from jax import lax
from jax.experimental import pallas as pl
from jax.experimental.pallas import tpu as pltpu

F32 = jnp.float32
BF16 = jnp.bfloat16

EPS = 1e-6
CONV_WIDTH = 4
LRU_C = 8.0
ROPE_BASE = 10000.0
HEADS = 4
HEAD_DIM = 128
HG_CHUNK = 32
GD_CHUNK = 64
GD_BASE = 16
SUBLANES = 8
LANES = 128
VMEM_LIMIT_BYTES = 56 * 1024 * 1024


def _dot(a, b):
    return jnp.dot(a, b, preferred_element_type=F32)


def _dot_nt(a, b):
    return lax.dot_general(a, b, (((1,), (1,)), ((), ())), preferred_element_type=F32)


def _dot_tn(a, b):
    return lax.dot_general(a, b, (((0,), (0,)), ((), ())), preferred_element_type=F32)


def _bf(x):
    return x.astype(BF16)


def _dot_exact_rhs(sel_bf16, x):
    hi = _bf(x)
    r1 = x - hi.astype(F32)
    mid = _bf(r1)
    lo = _bf(r1 - mid.astype(F32))
    return _dot(sel_bf16, hi) + _dot(sel_bf16, mid) + _dot(sel_bf16, lo)


def _softplus(x):
    return jnp.maximum(x, 0.0) + jnp.log1p(jnp.exp(-jnp.abs(x)))


def _neg_expm1(x):
    t = jnp.tanh(0.5 * x)
    return -2.0 * t / (1.0 - t)


def _silu(x):
    return x * jax.nn.sigmoid(x)


def _rmsnorm(x, w):
    return x * lax.rsqrt(jnp.mean(x * x, axis=-1, keepdims=True) + EPS) * w


def _norm_modulate(x, w, shift, scale):
    return x * lax.rsqrt(jnp.mean(x * x, axis=-1, keepdims=True) + EPS) * (w * (1.0 + scale)) + shift


def _shift_rows(x, prev8, s):
    rolled = pltpu.roll(x, s, axis=0)
    row = lax.broadcasted_iota(jnp.int32, (SUBLANES, x.shape[1]), 0)
    top = jnp.where(row < s, pltpu.roll(prev8, s, axis=0), rolled[0:SUBLANES])
    return jnp.concatenate([top, rolled[SUBLANES:]], axis=0)


def _causal_conv(x, prev8, w_ref, bias):
    acc = x * w_ref[CONV_WIDTH - 1:CONV_WIDTH, :]
    if bias is not None:
        acc = acc + bias
    for s in range(1, CONV_WIDTH):
        acc = acc + _shift_rows(x, prev8, s) * w_ref[CONV_WIDTH - 1 - s:CONV_WIDTH - s, :]
    return acc


def _same_block(n, blk):
    r = lax.broadcasted_iota(jnp.int32, (n, n), 0)
    c = lax.broadcasted_iota(jnp.int32, (n, n), 1)
    sh = int(math.log2(blk))
    return lax.shift_right_logical(r, sh) == lax.shift_right_logical(c, sh)


def _block_masks(n, blk):
    r = lax.broadcasted_iota(jnp.int32, (n, n), 0)
    c = lax.broadcasted_iota(jnp.int32, (n, n), 1)
    same = _same_block(n, blk)
    return same & (c <= r), same & (c < r)


def _unit_lower_inverses(a_mats, n, blk, base):
    r = lax.broadcasted_iota(jnp.int32, (n, n), 0)
    c = lax.broadcasted_iota(jnp.int32, (n, n), 1)
    inner = _same_block(n, base)
    pw = [jnp.where(inner, a, 0.0) for a in a_mats]
    t_inv = [jnp.where(r == c, 1.0, 0.0) - p for p in pw]
    for _ in range(int(math.log2(base)) - 1):
        pwb = [_bf(p) for p in pw]
        pw = [_dot(p, p) for p in pwb]
        t_inv = [t + _dot(_bf(t), _bf(p)) for t, p in zip(t_inv, pw)]
    size = base
    while size < blk:
        size *= 2
        outer = _same_block(n, size)
        off = [_bf(jnp.where(outer & jnp.logical_not(inner), a, 0.0)) for a in a_mats]
        tb = [_bf(t) for t in t_inv]
        t_inv = [t - _dot(b, _bf(_dot(o, b))) for t, b, o in zip(t_inv, tb, off)]
        inner = outer
    return t_inv


def _mod_kernel(c_ref, w_ref, b_ref, o_ref):
    c = c_ref[...]
    o_ref[0] = _dot(_bf(_silu(c)), _bf(w_ref[0])) + b_ref[0]


def _adaln_mod(c, ada_w, ada_b, tn=1536):
    depth, d, n = ada_w.shape
    b = c.shape[0]
    return pl.pallas_call(
        _mod_kernel,
        grid=(depth, n // tn),
        in_specs=[pl.BlockSpec((b, d), lambda l, j: (0, 0)),
                  pl.BlockSpec((1, d, tn), lambda l, j: (l, 0, j)),
                  pl.BlockSpec((1, 1, tn), lambda l, j: (l, 0, j))],
        out_specs=pl.BlockSpec((1, b, tn), lambda l, j: (l, 0, j)),
        out_shape=jax.ShapeDtypeStruct((depth, b, n), F32),
        compiler_params=pltpu.CompilerParams(
            dimension_semantics=("arbitrary", "arbitrary"), vmem_limit_bytes=VMEM_LIMIT_BYTES),
        name="adaln_mod",
    )(c, ada_w, ada_b.reshape(depth, 1, n))


def _lru_scan(a_s, u_s, h0, ts):
    width = a_s.shape[1]
    row = lax.broadcasted_iota(jnp.int32, (SUBLANES, width), 0)

    hprev = h0
    for g in range(ts // SUBLANES):
        rows = slice(g * SUBLANES, (g + 1) * SUBLANES)
        a8 = a_s[rows, :]
        u8 = u_s[rows, :]
        for d in (1, 2, 4):
            keep = row >= d
            a_sh = jnp.where(keep, pltpu.roll(a8, d, axis=0), 1.0)
            u_sh = jnp.where(keep, pltpu.roll(u8, d, axis=0), 0.0)
            u8 = u8 + a8 * u_sh
            a8 = a8 * a_sh
        h8 = u8 + a8 * hprev
        u_s[rows, :] = h8
        hprev = h8[SUBLANES - 1:SUBLANES, :]
    return hprev


def _even_kernel(x_ref, mod_ref, pos_ref, coff_ref, soff_ref, inv_ref, sgn_ref, nw_ref, win_ref, cw_ref,
                 cb_ref, wga_ref, wgx_ref, ba_ref, bx_ref, lam_ref, wout_ref, o_ref,
                 proj_all, a_s, u_s, mix_all, cos_s, sin_s, xprev_s, hprev_s, state_s, *, ts, chunk):
    lw = a_s.shape[1]
    half = lw // 2
    hd = HEAD_DIM
    n_sub = proj_all.shape[0] // ts

    def mix_tile(proj_s, mix_s, cos_view, sin_view):
        xr = proj_s[:, 0:lw]
        xc = _causal_conv(xr, xprev_s[...], cw_ref, cb_ref[...])
        xprev_s[...] = xr[ts - SUBLANES:ts, :]
        xcb = _bf(xc)
        r_pre = jnp.concatenate([_dot(xcb[:, :half], wga_ref[0]), _dot(xcb[:, half:], wga_ref[1])], axis=1)
        i_pre = jnp.concatenate([_dot(xcb[:, :half], wgx_ref[0]), _dot(xcb[:, half:], wgx_ref[1])], axis=1)
        r = jax.nn.sigmoid(r_pre + ba_ref[...])
        i = jax.nn.sigmoid(i_pre + bx_ref[...])
        log_a = (-LRU_C) * r * _softplus(-lam_ref[...])
        a_s[...] = jnp.exp(log_a)
        u_s[...] = jnp.sqrt(_neg_expm1(2.0 * log_a)) * (i * xc)
        hprev_s[...] = _lru_scan(a_s, u_s, hprev_s[...], ts)
        yr = proj_s[:, lw:2 * lw]
        mix_s[:, 0:lw] = _bf(u_s[...] * jax.nn.gelu(yr, approximate=True))

        cos_p, sin_p = cos_view[...], sin_view[...]
        lane = lax.broadcasted_iota(jnp.int32, (ts // 2, hd), 1)
        first = lane < hd // 2
        cos_r, sin_r = pltpu.roll(cos_p, hd // 2, axis=1), pltpu.roll(sin_p, hd // 2, axis=1)
        cos_d = jnp.concatenate([jnp.where(first, cos_p, cos_r), jnp.where(first, cos_r, cos_p)], axis=0)
        sin_d = jnp.concatenate([jnp.where(first, sin_p, sin_r), jnp.where(first, sin_r, sin_p)], axis=0)
        sin_d = sin_d * sgn_ref[...]
        ridx = lax.broadcasted_iota(jnp.int32, (chunk, chunk), 0)
        cidx = lax.broadcasted_iota(jnp.int32, (chunk, chunk), 1)
        rel = (ridx - cidx).astype(F32)
        tcol = lax.broadcasted_iota(jnp.int32, (chunk, 1), 0).astype(F32)
        qoff, koff, voff, goff = 2 * lw, 2 * lw + HEADS * hd, 2 * lw + 2 * HEADS * hd, 2 * lw + 3 * HEADS * hd
        log_gamma = [math.log1p(-(2.0 ** (-5.0 - hh))) for hh in range(HEADS)]
        dmats = [jnp.where(rel >= 0, jnp.exp(lg * jnp.maximum(rel, 0.0)), 0.0) for lg in log_gamma]
        q_decs = [jnp.exp(lg * (tcol + 1.0)) for lg in log_gamma]
        k_decs = [jnp.exp(lg * (chunk - 1.0 - tcol)) for lg in log_gamma]
        n_ret = ts // chunk
        pairs = [(cc, hh) for cc in range(n_ret) for hh in range(HEADS)]
        qb, vb, sc, kv = {}, {}, {}, {}
        for cc, hh in pairs:
            rows = slice(cc * chunk, (cc + 1) * chunk)
            cos_c, sin_c = cos_d[rows], sin_d[rows]
            qc = proj_s[rows, qoff + hh * hd:qoff + (hh + 1) * hd]
            kc = proj_s[rows, koff + hh * hd:koff + (hh + 1) * hd]
            qc = qc * cos_c + pltpu.roll(qc, hd // 2, axis=1) * sin_c
            kc = (kc * cos_c + pltpu.roll(kc, hd // 2, axis=1) * sin_c) * (hd ** -0.5)
            vb[cc, hh] = _bf(proj_s[rows, voff + hh * hd:voff + (hh + 1) * hd])
            qb[cc, hh] = _bf(qc)
            sc[cc, hh] = _bf(_dot_nt(qb[cc, hh], _bf(kc)) * dmats[hh])
            kv[cc, hh] = _dot_tn(_bf(kc * k_decs[hh]), vb[cc, hh])
        st_in = {}
        for hh, lg in enumerate(log_gamma):
            st = state_s[hh]
            for cc in range(n_ret):
                st_in[cc, hh] = _bf(st)
                st = math.exp(lg * chunk) * st + kv[cc, hh]
            state_s[hh] = st
        outs = {p: _dot(sc[p], vb[p]) + _dot(qb[p], st_in[p]) * q_decs[p[1]] for p in pairs}
        for cc, hh in pairs:
            rows = slice(cc * chunk, (cc + 1) * chunk)
            o = outs[cc, hh]
            mu = jnp.mean(o, axis=-1, keepdims=True)
            oc = o - mu
            var = jnp.mean(oc * oc, axis=-1, keepdims=True)
            g = proj_s[rows, goff + hh * hd:goff + (hh + 1) * hd]
            mix_s[rows, lw + hh * hd:lw + (hh + 1) * hd] = _bf(oc * lax.rsqrt(var + EPS) * _silu(g))

    @pl.when(pl.program_id(1) == 0)
    def _():
        xprev_s[...] = jnp.zeros_like(xprev_s)
        hprev_s[...] = jnp.zeros_like(hprev_s)
        state_s[...] = jnp.zeros_like(state_s)

    pos = pos_ref[0]
    base = pos[0:1, 0:1]
    row_i = lax.broadcasted_iota(jnp.int32, pos.shape, 0)
    lane_i = lax.broadcasted_iota(jnp.int32, pos.shape, 1)
    dev = jnp.abs(pos - base - (row_i * LANES + lane_i).astype(F32))
    dev = jnp.max(jnp.max(dev, axis=1, keepdims=True), axis=0, keepdims=True)
    consecutive = dev[0, 0] == 0.0

    @pl.when(consecutive)
    def _():
        ang0 = base * inv_ref[...]
        c0, s0 = jnp.cos(ang0), jnp.sin(ang0)
        cos_s[...] = c0 * coff_ref[...] - s0 * soff_ref[...]
        sin_s[...] = s0 * coff_ref[...] + c0 * soff_ref[...]

    @pl.when(jnp.logical_not(consecutive))
    def _():
        pos_t = pos.T
        first_half = lax.broadcasted_iota(jnp.int32, (ts // 2, hd), 1) < hd // 2
        for t in range(n_sub):
            packed = pl.ds(t * (ts // 2), ts // 2)
            ang = jnp.where(first_half, pos_t[:, 2 * t:2 * t + 1], pos_t[:, 2 * t + 1:2 * t + 2]) * inv_ref[...]
            cos_s[packed, :] = jnp.cos(ang)
            sin_s[packed, :] = jnp.sin(ang)

    x = x_ref[0]
    shift, scale, gate = mod_ref[0, 0:1, :], mod_ref[0, 1:2, :], mod_ref[0, 2:3, :]
    proj_all[...] = _dot(_bf(_norm_modulate(x, nw_ref[...], shift, scale)), win_ref[...])
    for t in range(n_sub):
        rows = pl.ds(t * ts, ts)
        packed = pl.ds(t * (ts // 2), ts // 2)
        mix_tile(proj_all.at[rows], mix_all.at[rows], cos_s.at[packed], sin_s.at[packed])
    o_ref[0] = x + gate * _dot(mix_all[...], wout_ref[...])


def _blockdiag_halves(w):
    n, bi, bj = w.shape
    eye = jnp.eye(n // 2, dtype=w.dtype)
    halves = []
    for p in range(2):
        blocks = w[p * (n // 2):(p + 1) * (n // 2)]
        dense = jnp.einsum('nm,nij->nimj', eye, blocks).reshape(n // 2 * bi, n // 2 * bj)
        halves.append(dense)
    return jnp.stack(halves)


def _const_spec(shape):
    nd = len(shape)
    return pl.BlockSpec(shape, lambda *_: (0,) * nd, pipeline_mode=pl.Buffered(1))


def _pack_half_tiles(v, ts):
    lead = v.shape[:-1]
    p = v.reshape(lead + (v.shape[-1] // ts, 2, ts // 2))
    p = jnp.swapaxes(p, -1, -2)
    p = jnp.repeat(p, HEAD_DIM // 2, axis=-1)
    return p.reshape(lead + (v.shape[-1] // 2, HEAD_DIM))


def _even_mixer(x, mod, positions, norm_w, w_in, conv_w, conv_b, w_a, b_a, w_x, b_x, lam, w_out,
                ts=256, n_sub=4, chunk=128):
    bsz, seq, d = x.shape
    lw = lam.shape[-1]
    n_in = w_in.shape[1]
    half = HEAD_DIM // 2
    inv = ROPE_BASE ** (-jnp.arange(half, dtype=F32) / half)
    inv2 = jnp.concatenate([inv, inv]).reshape(1, HEAD_DIM)
    sgn = jnp.concatenate([-jnp.ones((half,), F32), jnp.ones((half,), F32)]).reshape(1, HEAD_DIM)
    row = lambda v: v.reshape(1, -1).astype(F32)
    rows = n_sub * ts
    assert ts == 2 * LANES, "the packed rotary layout pairs the two 128-row halves of a tile"
    pos_rows = positions.astype(F32).reshape(bsz, seq // LANES, LANES)
    off_packed = _pack_half_tiles(jnp.arange(rows, dtype=F32), ts)
    cos_off, sin_off = jnp.cos(off_packed * inv2), jnp.sin(off_packed * inv2)
    kern = functools.partial(_even_kernel, ts=ts, chunk=chunk)
    return pl.pallas_call(
        kern,
        grid=(bsz, seq // rows),
        in_specs=[pl.BlockSpec((1, rows, d), lambda b, j: (b, j, 0)),
                  pl.BlockSpec((1, 6, d), lambda b, j: (b, 0, 0)),
                  pl.BlockSpec((1, rows // LANES, LANES), lambda b, j: (b, j, 0)),
                  _const_spec((rows // 2, HEAD_DIM)), _const_spec((rows // 2, HEAD_DIM)),
                  _const_spec((1, HEAD_DIM)), _const_spec((1, HEAD_DIM)), _const_spec((1, d)),
                  _const_spec((d, n_in)), _const_spec((CONV_WIDTH, lw)), _const_spec((1, lw)),
                  _const_spec((2, lw // 2, lw // 2)), _const_spec((2, lw // 2, lw // 2)),
                  _const_spec((1, lw)), _const_spec((1, lw)), _const_spec((1, lw)),
                  _const_spec((lw + HEADS * HEAD_DIM, d))],
        out_specs=pl.BlockSpec((1, rows, d), lambda b, j: (b, j, 0)),
        out_shape=jax.ShapeDtypeStruct(x.shape, F32),
        scratch_shapes=[pltpu.VMEM((rows, n_in), F32), pltpu.VMEM((ts, lw), F32), pltpu.VMEM((ts, lw), F32),
                        pltpu.VMEM((rows, lw + HEADS * HEAD_DIM), BF16),
                        pltpu.VMEM((rows // 2, HEAD_DIM), F32), pltpu.VMEM((rows // 2, HEAD_DIM), F32),
                        pltpu.VMEM((SUBLANES, lw), F32),
                        pltpu.VMEM((1, lw), F32), pltpu.VMEM((HEADS, HEAD_DIM, HEAD_DIM), F32)],
        compiler_params=pltpu.CompilerParams(
            dimension_semantics=("arbitrary", "arbitrary"), vmem_limit_bytes=VMEM_LIMIT_BYTES),
        name="even_mixer",
    )(x, mod, pos_rows, cos_off, sin_off, inv2, sgn, row(norm_w), _bf(w_in), conv_w.astype(F32), row(conv_b),
      _bf(_blockdiag_halves(w_a)), _bf(_blockdiag_halves(w_x)), row(b_a), row(b_x), row(lam), _bf(w_out))


def _odd_kernel(x_ref, mod_ref, nw_ref, win_ref, wtail_ref, lbl_ref, hgw_ref, cw_ref, alog_ref, dtb_ref, gdw_ref,
                wout_ref, o_ref, proj_all, b_s, qi_s, ki_s, mix_all, cprev_s, hg_state_s, gd_state_s,
                *, ts, layer):
    hd = HEAD_DIM
    hw = HEADS * hd
    n_sub = proj_all.shape[0] // ts

    def mix_tile(proj_s, mix_s):
        logits = lbl_ref[...]
        e = jnp.exp(logits - jnp.max(logits, axis=0, keepdims=True))
        lb = jnp.sum(e[1:layer + 1], axis=0, keepdims=True) / jnp.sum(e, axis=0, keepdims=True)
        key = (1.0 - lb) * jax.nn.sigmoid(-proj_s[:, hw:2 * hw])
        log_f = jnp.log1p(-key)
        incl32, _ = _block_masks(ts, HG_CHUNK)
        b_s[...] = _dot_exact_rhs(jnp.where(incl32, 1.0, 0.0).astype(BF16), log_f)
        mid = HG_CHUNK // 2
        n_hg = ts // HG_CHUNK
        for cc in range(n_hg):
            rows = slice(cc * HG_CHUNK, (cc + 1) * HG_CHUNK)
            b = b_s[rows, :]
            b_mid = b[mid:mid + 1, :]
            qi_s[rows, :] = _bf(proj_s[rows, 0:hw] * jnp.exp(b - b_mid))
            ki_s[rows, :] = _bf(key[rows] * jnp.exp(b_mid - b))
        head_cols = [slice(hh * hd, (hh + 1) * hd) for hh in range(HEADS)]
        hg_v = [_bf(proj_s[:, 2 * hw + hh * hd:2 * hw + (hh + 1) * hd]) for hh in range(HEADS)]
        hg_scores = [jnp.where(incl32, _dot_nt(qi_s[:, cols], ki_s[:, cols]), 0.0) for cols in head_cols]
        hg_intra = [_dot(_bf(sc), v) for sc, v in zip(hg_scores, hg_v)]
        hg_pairs = [(cc, hh) for cc in range(n_hg) for hh in range(HEADS)]
        hg_rows = [slice(cc * HG_CHUNK, (cc + 1) * HG_CHUNK) for cc in range(n_hg)]
        hg_kv, hg_dec, hg_qs = {}, {}, {}
        for cc, hh in hg_pairs:
            rows, cols = hg_rows[cc], head_cols[hh]
            b = b_s[rows, cols]
            b_last = b[HG_CHUNK - 1:HG_CHUNK, :]
            hg_dec[cc, hh] = jnp.exp(b_last)
            hg_qs[cc, hh] = _bf(proj_s[rows, cols] * jnp.exp(b))
            hg_kv[cc, hh] = _dot_tn(hg_v[hh][rows], _bf(key[rows, cols] * jnp.exp(b_last - b)))
        hg_in = {}
        for hh in range(HEADS):
            st_t = hg_state_s[hh]
            for cc in range(n_hg):
                hg_in[cc, hh] = _bf(st_t)
                st_t = st_t * hg_dec[cc, hh] + hg_kv[cc, hh]
            hg_state_s[hh] = st_t
        hg_out = {p: hg_intra[p[1]][hg_rows[p[0]]] + _dot_nt(hg_qs[p], hg_in[p]) for p in hg_pairs}
        for cc, hh in hg_pairs:
            rows = hg_rows[cc]
            o = hg_out[cc, hh]
            on = o * lax.rsqrt(jnp.mean(o * o, axis=-1, keepdims=True) + EPS) * hgw_ref[...]
            gh = proj_s[rows, 3 * hw + hh * hd:3 * hw + (hh + 1) * hd]
            mix_s[rows, head_cols[hh]] = _bf(on * _silu(gh))

        qkv_raw = proj_s[:, 4 * hw:7 * hw]
        qkv = _silu(_causal_conv(qkv_raw, cprev_s[...], cw_ref, None))
        cprev_s[...] = qkv_raw[ts - SUBLANES:ts, :]
        ab = proj_s[:, 8 * hw:8 * hw + LANES]
        g_all = -jnp.exp(alog_ref[...]) * _softplus(ab + dtb_ref[...])
        beta_all = jax.nn.sigmoid(ab)
        incl64, strict64 = _block_masks(ts, GD_CHUNK)
        gcum = _dot_exact_rhs(jnp.where(incl64, 1.0, 0.0).astype(BF16), g_all)
        gcum_t = gcum.T
        n_gd = ts // GD_CHUNK
        gd_q, gd_k, gd_gcol, gd_qk, a_mats, rhs_all = [], [], [], [], [], []
        for hh in range(HEADS):
            qh = qkv[:, hh * hd:(hh + 1) * hd]
            kh = qkv[:, hw + hh * hd:hw + (hh + 1) * hd]
            vh = qkv[:, 2 * hw + hh * hd:2 * hw + (hh + 1) * hd]
            qh = qh * lax.rsqrt(jnp.sum(qh * qh, axis=-1, keepdims=True) + EPS) * (hd ** -0.5)
            kh = kh * lax.rsqrt(jnp.sum(kh * kh, axis=-1, keepdims=True) + EPS)
            gcol = gcum[:, hh:hh + 1]
            grow = gcum_t[hh:hh + 1, :]
            beta = beta_all[:, HEADS + hh:HEADS + hh + 1]
            decay = jnp.where(incl64, jnp.exp(jnp.where(incl64, gcol - grow, 0.0)), 0.0)
            kb = kh * beta
            khb = _bf(kh)
            a_mats.append(jnp.where(strict64, _dot_nt(_bf(kb), khb) * decay, 0.0))
            gd_qk.append(_bf(jnp.where(incl64, _dot_nt(_bf(qh), khb) * decay, 0.0)))
            egc = jnp.exp(gcol)
            rhs_all.append(_bf(jnp.concatenate([vh * beta, kb * egc], axis=1)))
            gd_q.append(_bf(qh * egc))
            gd_k.append(kh)
            gd_gcol.append(gcol)
        t_inv = _unit_lower_inverses(a_mats, ts, GD_CHUNK, GD_BASE)
        sol = [_dot(_bf(t), rh) for t, rh in zip(t_inv, rhs_all)]
        chunk_rows = [slice(cc * GD_CHUNK, (cc + 1) * GD_CHUNK) for cc in range(n_gd)]
        gd_states = [[None] * n_gd for _ in range(HEADS)]
        ktu = [[None] * n_gd for _ in range(HEADS)]
        ktw = [[None] * n_gd for _ in range(HEADS)]
        g_last = [[None] * n_gd for _ in range(HEADS)]
        for cc, rows in enumerate(chunk_rows):
            for hh in range(HEADS):
                gcol = gd_gcol[hh]
                g_last[hh][cc] = gcol[(cc + 1) * GD_CHUNK - 1:(cc + 1) * GD_CHUNK, :]
                k_g = _bf(gd_k[hh][rows] * jnp.exp(g_last[hh][cc] - gcol[rows]))
                ktu[hh][cc] = _dot_tn(k_g, _bf(sol[hh][rows, :hd]))
                ktw[hh][cc] = _bf(_dot_tn(k_g, _bf(sol[hh][rows, hd:])))
        st = [gd_state_s[hh] for hh in range(HEADS)]
        for cc in range(n_gd):
            for hh in range(HEADS):
                gd_states[hh][cc] = _bf(st[hh])
                st[hh] = st[hh] * jnp.exp(g_last[hh][cc]) + ktu[hh][cc] - _dot(ktw[hh][cc], gd_states[hh][cc])
        for hh in range(HEADS):
            gd_state_s[hh] = st[hh]
        gd_pairs = [(cc, hh) for cc in range(n_gd) for hh in range(HEADS)]
        wq = {(cc, hh): _dot(jnp.concatenate([_bf(sol[hh][chunk_rows[cc], hd:]), gd_q[hh][chunk_rows[cc]]], axis=0),
                             gd_states[hh][cc]) for cc, hh in gd_pairs}
        v_newb = {(cc, hh): _bf(sol[hh][chunk_rows[cc], :hd] - wq[cc, hh][:GD_CHUNK]) for cc, hh in gd_pairs}
        gd_out = {(cc, hh): wq[cc, hh][GD_CHUNK:] + _dot(gd_qk[hh][chunk_rows[cc], chunk_rows[cc]], v_newb[cc, hh])
                  for cc, hh in gd_pairs}
        for cc, hh in gd_pairs:
            rows = chunk_rows[cc]
            o = gd_out[cc, hh]
            on = o * lax.rsqrt(jnp.mean(o * o, axis=-1, keepdims=True) + EPS) * gdw_ref[...]
            zh = proj_s[rows, 7 * hw + hh * hd:7 * hw + (hh + 1) * hd]
            mix_s[rows, hw + hh * hd:hw + (hh + 1) * hd] = _bf(on * _silu(zh))

    @pl.when(pl.program_id(1) == 0)
    def _():
        cprev_s[...] = jnp.zeros_like(cprev_s)
        hg_state_s[...] = jnp.zeros_like(hg_state_s)
        gd_state_s[...] = jnp.zeros_like(gd_state_s)

    x = x_ref[0]
    shift, scale, gate = mod_ref[0, 0:1, :], mod_ref[0, 1:2, :], mod_ref[0, 2:3, :]
    hb = _bf(_norm_modulate(x, nw_ref[...], shift, scale))
    proj_all[:, 0:8 * hw] = _dot(hb, win_ref[...])
    proj_all[:, 8 * hw:] = _dot(hb, wtail_ref[...])
    for t in range(n_sub):
        rows = pl.ds(t * ts, ts)
        mix_tile(proj_all.at[rows], mix_all.at[rows])
    o_ref[0] = x + gate * _dot(mix_all[...], wout_ref[...])


def _odd_mixer(x, mod, norm_w, lb_logits, layer, w_in, hg_norm_w, conv_w, a_log, dt_bias, gd_norm_w, w_out,
               ts=256, n_sub=2):
    bsz, seq, d = x.shape
    hw = HEADS * HEAD_DIM
    n_in = 8 * hw + LANES
    w_main = _bf(w_in[:, :8 * hw])
    w_tail = jnp.zeros((d, LANES), BF16).at[:, :w_in.shape[1] - 8 * hw].set(_bf(w_in[:, 8 * hw:]))
    lane_pad = lambda v, off: jnp.zeros((1, LANES), F32).at[0, off:off + v.shape[0]].set(v.astype(F32))
    row = lambda v: v.reshape(1, -1).astype(F32)
    depth = lb_logits.shape[0]
    rows = n_sub * ts
    kern = functools.partial(_odd_kernel, ts=ts, layer=layer)
    return pl.pallas_call(
        kern,
        grid=(bsz, seq // rows),
        in_specs=[pl.BlockSpec((1, rows, d), lambda b, j: (b, j, 0)),
                  pl.BlockSpec((1, 6, d), lambda b, j: (b, 0, 0)),
                  _const_spec((1, d)), _const_spec((d, 8 * hw)), _const_spec((d, LANES)), _const_spec((depth, hw)),
                  _const_spec((1, HEAD_DIM)), _const_spec((CONV_WIDTH, 3 * hw)),
                  _const_spec((1, LANES)), _const_spec((1, LANES)), _const_spec((1, HEAD_DIM)),
                  _const_spec((2 * hw, d))],
        out_specs=pl.BlockSpec((1, rows, d), lambda b, j: (b, j, 0)),
        out_shape=jax.ShapeDtypeStruct(x.shape, F32),
        scratch_shapes=[pltpu.VMEM((rows, n_in), F32), pltpu.VMEM((ts, hw), F32),
                        pltpu.VMEM((ts, hw), BF16), pltpu.VMEM((ts, hw), BF16),
                        pltpu.VMEM((rows, 2 * hw), BF16), pltpu.VMEM((SUBLANES, 3 * hw), F32),
                        pltpu.VMEM((HEADS, HEAD_DIM, HEAD_DIM), F32), pltpu.VMEM((HEADS, HEAD_DIM, HEAD_DIM), F32)],
        compiler_params=pltpu.CompilerParams(
            dimension_semantics=("arbitrary", "arbitrary"), vmem_limit_bytes=VMEM_LIMIT_BYTES),
        name="odd_mixer",
    )(x, mod, row(norm_w), w_main, w_tail, lb_logits.astype(F32), row(hg_norm_w), conv_w.astype(F32),
      lane_pad(a_log, 0), lane_pad(dt_bias, 0), row(gd_norm_w), _bf(w_out))


def _mlp_kernel(x_ref, mod_ref, nw_ref, w1_ref, w2_ref, fw_ref, o_ref, *, n_split, final):
    x = x_ref[0]
    shift, scale, gate = mod_ref[0, 3:4, :], mod_ref[0, 4:5, :], mod_ref[0, 5:6, :]
    h = _bf(_norm_modulate(x, nw_ref[...], shift, scale))
    hidden = w1_ref.shape[2]
    step = hidden // n_split
    acc = jnp.zeros(x.shape, F32)
    for s in range(n_split):
        hid = jnp.square(jnp.maximum(_dot(h, w1_ref[0, :, s * step:(s + 1) * step]), 0.0))
        acc = acc + _dot(_bf(hid), w2_ref[0, s * step:(s + 1) * step, :])
    y = x + gate * acc
    if final:
        y = _rmsnorm(y, fw_ref[...])
    o_ref[0] = y


def _mlp(x, mod, norm_w, w1_all, w2_all, layer, final_w, final, tm=1024, n_split=4):
    bsz, seq, d = x.shape
    hidden = w1_all.shape[2]
    kern = functools.partial(_mlp_kernel, n_split=n_split, final=final)
    row = lambda v: v.reshape(1, -1).astype(F32)
    layer_block = lambda shape: pl.BlockSpec(shape, lambda b, j: (layer, 0, 0), pipeline_mode=pl.Buffered(1))
    return pl.pallas_call(
        kern,
        grid=(bsz, seq // tm),
        in_specs=[pl.BlockSpec((1, tm, d), lambda b, j: (b, j, 0)),
                  pl.BlockSpec((1, 6, d), lambda b, j: (b, 0, 0)),
                  _const_spec((1, d)), layer_block((1, d, hidden)), layer_block((1, hidden, d)),
                  _const_spec((1, d))],
        out_specs=pl.BlockSpec((1, tm, d), lambda b, j: (b, j, 0)),
        out_shape=jax.ShapeDtypeStruct(x.shape, F32),
        compiler_params=pltpu.CompilerParams(
            dimension_semantics=("arbitrary", "arbitrary"), vmem_limit_bytes=VMEM_LIMIT_BYTES),
        name="mlp_final" if final else "mlp",
    )(x, mod, row(norm_w), w1_all, w2_all, row(final_w))


def kernel(x, c, positions, ada_w, ada_b, norm_mix_w, norm_mlp_w, mlp_w1, mlp_w2, final_norm_w, ev_w_in, lru_conv_w, lru_conv_b, lru_w_a, lru_b_a, lru_w_x, lru_b_x, lru_lambda, ev_w_out, hg_lb_logits, od_w_in, hg_norm_w, gd_conv_w, gd_a_log, gd_dt_bias, gd_norm_w, od_w_out):
    depth = ada_w.shape[0]
    bsz, seq, d = x.shape
    mod = _adaln_mod(c, ada_w, ada_b).reshape(depth, bsz, 6, d)
    w1_all, w2_all = _bf(mlp_w1), _bf(mlp_w2)
    for layer in range(depth):
        j = layer // 2
        if layer % 2 == 0:
            x = _even_mixer(x, mod[layer], positions, norm_mix_w[layer], ev_w_in[j], lru_conv_w[j], lru_conv_b[j],
                            lru_w_a[j], lru_b_a[j], lru_w_x[j], lru_b_x[j], lru_lambda[j], ev_w_out[j])
        else:
            x = _odd_mixer(x, mod[layer], norm_mix_w[layer], hg_lb_logits, layer, od_w_in[j], hg_norm_w[j],
                           gd_conv_w[j], gd_a_log[j], gd_dt_bias[j], gd_norm_w[j], od_w_out[j])
        x = _mlp(x, mod[layer], norm_mlp_w[layer], w1_all, w2_all, layer, final_norm_w,
                 final=(layer == depth - 1))
    return x
```

```python
import functools
import math

import jax
import jax.numpy as jnp
from jax import lax
from jax.experimental import pallas as pl
from jax.experimental.pallas import tpu as pltpu

F32 = jnp.float32
BF16 = jnp.bfloat16

EPS = 1e-6
CONV_WIDTH = 4
LRU_C = 8.0
ROPE_BASE = 10000.0
HEADS = 4
HEAD_DIM = 128
HG_CHUNK = 32
GD_CHUNK = 64
GD_BASE = 16
SUBLANES = 8
LANES = 128
VMEM_LIMIT_BYTES = 56 * 1024 * 1024


def _dot(a, b):
    return jnp.dot(a, b, preferred_element_type=F32)


def _dot_nt(a, b):
    return lax.dot_general(a, b, (((1,), (1,)), ((), ())), preferred_element_type=F32)


def _dot_tn(a, b):
    return lax.dot_general(a, b, (((0,), (0,)), ((), ())), preferred_element_type=F32)


def _bf(x):
    return x.astype(BF16)


def _dot_exact_rhs(sel_bf16, x):
    hi = _bf(x)
    lo = _bf(x - hi.astype(F32))
    return _dot(sel_bf16, hi) + _dot(sel_bf16, lo)


def _softplus(x):
    return jnp.maximum(x, 0.0) + jnp.log1p(jnp.exp(-jnp.abs(x)))


def _neg_expm1(x):
    t = jnp.tanh(0.5 * x)
    return -2.0 * t / (1.0 - t)


def _silu(x):
    return x * jax.nn.sigmoid(x)


def _rmsnorm(x, w):
    return x * lax.rsqrt(jnp.mean(x * x, axis=-1, keepdims=True) + EPS) * w


def _norm_modulate(x, w, shift, scale):
    return x * lax.rsqrt(jnp.mean(x * x, axis=-1, keepdims=True) + EPS) * (w * (1.0 + scale)) + shift


def _causal_conv(x, prev8, w_ref, bias):
    n, c = x.shape
    x_prev = jnp.concatenate([prev8, x[:n - SUBLANES]], axis=0)
    sub = lax.broadcasted_iota(jnp.int32, (n, c), 0) & (SUBLANES - 1)
    acc = x * w_ref[CONV_WIDTH - 1:CONV_WIDTH, :]
    if bias is not None:
        acc = acc + bias
    for s in range(1, CONV_WIDTH):
        z = jnp.where(sub >= SUBLANES - s, x_prev, x)
        delayed = pltpu.roll(z.reshape(n // SUBLANES, SUBLANES, c), s, axis=1).reshape(n, c)
        acc = acc + delayed * w_ref[CONV_WIDTH - 1 - s:CONV_WIDTH - s, :]
    return acc


def _same_block(n, blk):
    r = lax.broadcasted_iota(jnp.int32, (n, n), 0)
    c = lax.broadcasted_iota(jnp.int32, (n, n), 1)
    sh = int(math.log2(blk))
    return lax.shift_right_logical(r, sh) == lax.shift_right_logical(c, sh)


def _block_masks(n, blk):
    r = lax.broadcasted_iota(jnp.int32, (n, n), 0)
    c = lax.broadcasted_iota(jnp.int32, (n, n), 1)
    same = _same_block(n, blk)
    return same & (c <= r), same & (c < r)


def _unit_lower_inverses(a_mats, n, blk, base):
    r = lax.broadcasted_iota(jnp.int32, (n, n), 0)
    c = lax.broadcasted_iota(jnp.int32, (n, n), 1)
    inner = _same_block(n, base)
    pw = [jnp.where(inner, a, 0.0) for a in a_mats]
    t_inv = [jnp.where(r == c, 1.0, 0.0) - p for p in pw]
    for _ in range(int(math.log2(base)) - 1):
        pwb = [_bf(p) for p in pw]
        pw = [_dot(p, p) for p in pwb]
        t_inv = [t + _dot(_bf(t), _bf(p)) for t, p in zip(t_inv, pw)]
    size = base
    while size < blk:
        size *= 2
        outer = _same_block(n, size)
        off = [_bf(jnp.where(outer & jnp.logical_not(inner), a, 0.0)) for a in a_mats]
        tb = [_bf(t) for t in t_inv]
        t_inv = [t - _dot(b, _bf(_dot(o, b))) for t, b, o in zip(t_inv, tb, off)]
        inner = outer
    return t_inv


def _mod_kernel(c_ref, w_ref, b_ref, o_ref):
    c = c_ref[...]
    o_ref[0] = _dot(_bf(_silu(c)), _bf(w_ref[0])) + b_ref[0]


def _adaln_mod(c, ada_w, ada_b, tn=1536):
    depth, d, n = ada_w.shape
    b = c.shape[0]
    return pl.pallas_call(
        _mod_kernel,
        grid=(depth, n // tn),
        in_specs=[pl.BlockSpec((b, d), lambda l, j: (0, 0)),
                  pl.BlockSpec((1, d, tn), lambda l, j: (l, 0, j)),
                  pl.BlockSpec((1, 1, tn), lambda l, j: (l, 0, j))],
        out_specs=pl.BlockSpec((1, b, tn), lambda l, j: (l, 0, j)),
        out_shape=jax.ShapeDtypeStruct((depth, b, n), F32),
        compiler_params=pltpu.CompilerParams(
            dimension_semantics=("arbitrary", "arbitrary"), vmem_limit_bytes=VMEM_LIMIT_BYTES),
        name="adaln_mod",
    )(c, ada_w, ada_b.reshape(depth, 1, n))


def _lru_scan(a_s, u_s, h0, ts):
    width = a_s.shape[1]
    row = lax.broadcasted_iota(jnp.int32, (SUBLANES, width), 0)

    hprev = h0
    for g in range(ts // SUBLANES):
        rows = slice(g * SUBLANES, (g + 1) * SUBLANES)
        a8 = a_s[rows, :]
        u8 = u_s[rows, :]
        for d in (1, 2, 4):
            keep = row >= d
            a_sh = jnp.where(keep, pltpu.roll(a8, d, axis=0), 1.0)
            u_sh = jnp.where(keep, pltpu.roll(u8, d, axis=0), 0.0)
            u8 = u8 + a8 * u_sh
            a8 = a8 * a_sh
        h8 = u8 + a8 * hprev
        u_s[rows, :] = h8
        hprev = h8[SUBLANES - 1:SUBLANES, :]
    return hprev


def _even_kernel(x_ref, mod_ref, pos_ref, coff_ref, soff_ref, inv_ref, sgn_ref, nw_ref, win_ref, cw_ref,
                 cb_ref, wga_ref, wgx_ref, ba_ref, bx_ref, lam_ref, wout_ref, o_ref,
                 proj_all, a_s, u_s, mix_all, cos_s, sin_s, xprev_s, hprev_s, state_s, *, ts, chunk):
    lw = a_s.shape[1]
    half = lw // 2
    hd = HEAD_DIM
    n_sub = proj_all.shape[0] // ts

    def mix_tile(proj_s, mix_s, cos_view, sin_view):
        xr = proj_s[:, 0:lw]
        xc = _causal_conv(xr, xprev_s[...], cw_ref, cb_ref[...])
        xprev_s[...] = xr[ts - SUBLANES:ts, :]
        xcb = _bf(xc)
        r_pre = jnp.concatenate([_dot(xcb[:, :half], wga_ref[0]), _dot(xcb[:, half:], wga_ref[1])], axis=1)
        i_pre = jnp.concatenate([_dot(xcb[:, :half], wgx_ref[0]), _dot(xcb[:, half:], wgx_ref[1])], axis=1)
        r = jax.nn.sigmoid(r_pre + ba_ref[...])
        i = jax.nn.sigmoid(i_pre + bx_ref[...])
        log_a = (-LRU_C) * r * _softplus(-lam_ref[...])
        a_s[...] = jnp.exp(log_a)
        u_s[...] = jnp.sqrt(_neg_expm1(2.0 * log_a)) * (i * xc)
        hprev_s[...] = _lru_scan(a_s, u_s, hprev_s[...], ts)
        yr = proj_s[:, lw:2 * lw]
        mix_s[:, 0:lw] = _bf(u_s[...] * jax.nn.gelu(yr, approximate=True))

        cos_p, sin_p = cos_view[...], sin_view[...]
        lane = lax.broadcasted_iota(jnp.int32, (ts // 2, hd), 1)
        first = lane < hd // 2
        cos_r, sin_r = pltpu.roll(cos_p, hd // 2, axis=1), pltpu.roll(sin_p, hd // 2, axis=1)
        cos_d = jnp.concatenate([jnp.where(first, cos_p, cos_r), jnp.where(first, cos_r, cos_p)], axis=0)
        sin_d = jnp.concatenate([jnp.where(first, sin_p, sin_r), jnp.where(first, sin_r, sin_p)], axis=0)
        sin_d = sin_d * sgn_ref[...]
        ridx = lax.broadcasted_iota(jnp.int32, (chunk, chunk), 0)
        cidx = lax.broadcasted_iota(jnp.int32, (chunk, chunk), 1)
        rel = (ridx - cidx).astype(F32)
        tcol = lax.broadcasted_iota(jnp.int32, (chunk, 1), 0).astype(F32)
        qoff, koff, voff, goff = 2 * lw, 2 * lw + HEADS * hd, 2 * lw + 2 * HEADS * hd, 2 * lw + 3 * HEADS * hd
        log_gamma = [math.log1p(-(2.0 ** (-5.0 - hh))) for hh in range(HEADS)]
        dmats = [jnp.where(rel >= 0, jnp.exp(lg * jnp.maximum(rel, 0.0)), 0.0) for lg in log_gamma]
        q_decs = [jnp.exp(lg * (tcol + 1.0)) for lg in log_gamma]
        k_decs = [jnp.exp(lg * (chunk - 1.0 - tcol)) for lg in log_gamma]
        n_ret = ts // chunk
        pairs = [(cc, hh) for cc in range(n_ret) for hh in range(HEADS)]
        qb, vb, sc, kv = {}, {}, {}, {}
        for cc, hh in pairs:
            rows = slice(cc * chunk, (cc + 1) * chunk)
            cos_c, sin_c = cos_d[rows], sin_d[rows]
            qc = proj_s[rows, qoff + hh * hd:qoff + (hh + 1) * hd]
            kc = proj_s[rows, koff + hh * hd:koff + (hh + 1) * hd]
            qc = qc * cos_c + pltpu.roll(qc, hd // 2, axis=1) * sin_c
            kc = (kc * cos_c + pltpu.roll(kc, hd // 2, axis=1) * sin_c) * (hd ** -0.5)
            vb[cc, hh] = _bf(proj_s[rows, voff + hh * hd:voff + (hh + 1) * hd])
            qb[cc, hh] = _bf(qc)
            sc[cc, hh] = _bf(_dot_nt(qb[cc, hh], _bf(kc)) * dmats[hh])
            kv[cc, hh] = _dot_tn(_bf(kc * k_decs[hh]), vb[cc, hh])
        st_in = {}
        for hh, lg in enumerate(log_gamma):
            st = state_s[hh]
            for cc in range(n_ret):
                st_in[cc, hh] = _bf(st)
                st = math.exp(lg * chunk) * st + kv[cc, hh]
            state_s[hh] = st
        outs = {p: _dot(sc[p], vb[p]) + _dot(qb[p], st_in[p]) * q_decs[p[1]] for p in pairs}
        for cc, hh in pairs:
            rows = slice(cc * chunk, (cc + 1) * chunk)
            o = outs[cc, hh]
            mu = jnp.mean(o, axis=-1, keepdims=True)
            oc = o - mu
            var = jnp.mean(oc * oc, axis=-1, keepdims=True)
            g = proj_s[rows, goff + hh * hd:goff + (hh + 1) * hd]
            mix_s[rows, lw + hh * hd:lw + (hh + 1) * hd] = _bf(oc * lax.rsqrt(var + EPS) * _silu(g))

    @pl.when(pl.program_id(1) == 0)
    def _():
        xprev_s[...] = jnp.zeros_like(xprev_s)
        hprev_s[...] = jnp.zeros_like(hprev_s)
        state_s[...] = jnp.zeros_like(state_s)

    pos = pos_ref[0]
    base = pos[0:1, 0:1]
    row_i = lax.broadcasted_iota(jnp.int32, pos.shape, 0)
    lane_i = lax.broadcasted_iota(jnp.int32, pos.shape, 1)
    dev = jnp.abs(pos - base - (row_i * LANES + lane_i).astype(F32))
    dev = jnp.max(jnp.max(dev, axis=1, keepdims=True), axis=0, keepdims=True)
    consecutive = dev[0, 0] == 0.0

    @pl.when(consecutive)
    def _():
        ang0 = base * inv_ref[...]
        c0, s0 = jnp.cos(ang0), jnp.sin(ang0)
        cos_s[...] = c0 * coff_ref[...] - s0 * soff_ref[...]
        sin_s[...] = s0 * coff_ref[...] + c0 * soff_ref[...]

    @pl.when(jnp.logical_not(consecutive))
    def _():
        pos_t = pos.T
        first_half = lax.broadcasted_iota(jnp.int32, (ts // 2, hd), 1) < hd // 2
        for t in range(n_sub):
            packed = pl.ds(t * (ts // 2), ts // 2)
            ang = jnp.where(first_half, pos_t[:, 2 * t:2 * t + 1], pos_t[:, 2 * t + 1:2 * t + 2]) * inv_ref[...]
            cos_s[packed, :] = jnp.cos(ang)
            sin_s[packed, :] = jnp.sin(ang)

    x = x_ref[0]
    shift, scale, gate = mod_ref[0, 0:1, :], mod_ref[0, 1:2, :], mod_ref[0, 2:3, :]
    proj_all[...] = _dot(_bf(_norm_modulate(x, nw_ref[...], shift, scale)), win_ref[...])
    for t in range(n_sub):
        rows = pl.ds(t * ts, ts)
        packed = pl.ds(t * (ts // 2), ts // 2)
        mix_tile(proj_all.at[rows], mix_all.at[rows], cos_s.at[packed], sin_s.at[packed])
    o_ref[0] = x + gate * _dot(mix_all[...], wout_ref[...])


def _blockdiag_halves(w):
    n, bi, bj = w.shape
    eye = jnp.eye(n // 2, dtype=w.dtype)
    halves = []
    for p in range(2):
        blocks = w[p * (n // 2):(p + 1) * (n // 2)]
        dense = jnp.einsum('nm,nij->nimj', eye, blocks).reshape(n // 2 * bi, n // 2 * bj)
        halves.append(dense)
    return jnp.stack(halves)


def _const_spec(shape):
    nd = len(shape)
    return pl.BlockSpec(shape, lambda *_: (0,) * nd, pipeline_mode=pl.Buffered(1))


def _pack_half_tiles(v, ts):
    lead = v.shape[:-1]
    p = v.reshape(lead + (v.shape[-1] // ts, 2, ts // 2))
    p = jnp.swapaxes(p, -1, -2)
    p = jnp.repeat(p, HEAD_DIM // 2, axis=-1)
    return p.reshape(lead + (v.shape[-1] // 2, HEAD_DIM))


def _even_mixer(x, mod, positions, norm_w, w_in, conv_w, conv_b, w_a, b_a, w_x, b_x, lam, w_out,
                ts=256, n_sub=4, chunk=128):
    bsz, seq, d = x.shape
    lw = lam.shape[-1]
    n_in = w_in.shape[1]
    half = HEAD_DIM // 2
    inv = ROPE_BASE ** (-jnp.arange(half, dtype=F32) / half)
    inv2 = jnp.concatenate([inv, inv]).reshape(1, HEAD_DIM)
    sgn = jnp.concatenate([-jnp.ones((half,), F32), jnp.ones((half,), F32)]).reshape(1, HEAD_DIM)
    row = lambda v: v.reshape(1, -1).astype(F32)
    rows = n_sub * ts
    assert ts == 2 * LANES, "the packed rotary layout pairs the two 128-row halves of a tile"
    pos_rows = positions.astype(F32).reshape(bsz, seq // LANES, LANES)
    off_packed = _pack_half_tiles(jnp.arange(rows, dtype=F32), ts)
    cos_off, sin_off = jnp.cos(off_packed * inv2), jnp.sin(off_packed * inv2)
    kern = functools.partial(_even_kernel, ts=ts, chunk=chunk)
    return pl.pallas_call(
        kern,
        grid=(bsz, seq // rows),
        in_specs=[pl.BlockSpec((1, rows, d), lambda b, j: (b, j, 0)),
                  pl.BlockSpec((1, 6, d), lambda b, j: (b, 0, 0)),
                  pl.BlockSpec((1, rows // LANES, LANES), lambda b, j: (b, j, 0)),
                  _const_spec((rows // 2, HEAD_DIM)), _const_spec((rows // 2, HEAD_DIM)),
                  _const_spec((1, HEAD_DIM)), _const_spec((1, HEAD_DIM)), _const_spec((1, d)),
                  _const_spec((d, n_in)), _const_spec((CONV_WIDTH, lw)), _const_spec((1, lw)),
                  _const_spec((2, lw // 2, lw // 2)), _const_spec((2, lw // 2, lw // 2)),
                  _const_spec((1, lw)), _const_spec((1, lw)), _const_spec((1, lw)),
                  _const_spec((lw + HEADS * HEAD_DIM, d))],
        out_specs=pl.BlockSpec((1, rows, d), lambda b, j: (b, j, 0)),
        out_shape=jax.ShapeDtypeStruct(x.shape, F32),
        scratch_shapes=[pltpu.VMEM((rows, n_in), F32), pltpu.VMEM((ts, lw), F32), pltpu.VMEM((ts, lw), F32),
                        pltpu.VMEM((rows, lw + HEADS * HEAD_DIM), BF16),
                        pltpu.VMEM((rows // 2, HEAD_DIM), F32), pltpu.VMEM((rows // 2, HEAD_DIM), F32),
                        pltpu.VMEM((SUBLANES, lw), F32),
                        pltpu.VMEM((1, lw), F32), pltpu.VMEM((HEADS, HEAD_DIM, HEAD_DIM), F32)],
        compiler_params=pltpu.CompilerParams(
            dimension_semantics=("arbitrary", "arbitrary"), vmem_limit_bytes=VMEM_LIMIT_BYTES),
        name="even_mixer",
    )(x, mod, pos_rows, cos_off, sin_off, inv2, sgn, row(norm_w), _bf(w_in), conv_w.astype(F32), row(conv_b),
      _bf(_blockdiag_halves(w_a)), _bf(_blockdiag_halves(w_x)), row(b_a), row(b_x), row(lam), _bf(w_out))


def _odd_kernel(x_ref, mod_ref, nw_ref, win_ref, wtail_ref, lbl_ref, hgw_ref, cw_ref, alog_ref, dtb_ref, gdw_ref,
                wout_ref, o_ref, proj_all, b_s, qi_s, ki_s, mix_all, cprev_s, hg_state_s, gd_state_s,
                *, ts, layer):
    hd = HEAD_DIM
    hw = HEADS * hd
    n_sub = proj_all.shape[0] // ts

    def mix_tile(proj_s, mix_s):
        logits = lbl_ref[...]
        e = jnp.exp(logits - jnp.max(logits, axis=0, keepdims=True))
        lb = jnp.sum(e[1:layer + 1], axis=0, keepdims=True) / jnp.sum(e, axis=0, keepdims=True)
        key = (1.0 - lb) * jax.nn.sigmoid(-proj_s[:, hw:2 * hw])
        log_f = jnp.log1p(-key)
        incl32, _ = _block_masks(ts, HG_CHUNK)
        b_s[...] = _dot_exact_rhs(jnp.where(incl32, 1.0, 0.0).astype(BF16), log_f)
        mid = HG_CHUNK // 2
        n_hg = ts // HG_CHUNK
        for cc in range(n_hg):
            rows = slice(cc * HG_CHUNK, (cc + 1) * HG_CHUNK)
            b = b_s[rows, :]
            b_mid = b[mid:mid + 1, :]
            qi_s[rows, :] = _bf(proj_s[rows, 0:hw] * jnp.exp(b - b_mid))
            ki_s[rows, :] = _bf(key[rows] * jnp.exp(b_mid - b))
        head_cols = [slice(hh * hd, (hh + 1) * hd) for hh in range(HEADS)]
        hg_v = [_bf(proj_s[:, 2 * hw + hh * hd:2 * hw + (hh + 1) * hd]) for hh in range(HEADS)]
        hg_scores = [jnp.where(incl32, _dot_nt(qi_s[:, cols], ki_s[:, cols]), 0.0) for cols in head_cols]
        hg_intra = [_dot(_bf(sc), v) for sc, v in zip(hg_scores, hg_v)]
        hg_pairs = [(cc, hh) for cc in range(n_hg) for hh in range(HEADS)]
        hg_rows = [slice(cc * HG_CHUNK, (cc + 1) * HG_CHUNK) for cc in range(n_hg)]
        hg_kv, hg_dec, hg_qs = {}, {}, {}
        for cc, hh in hg_pairs:
            rows, cols = hg_rows[cc], head_cols[hh]
            b = b_s[rows, cols]
            b_last = b[HG_CHUNK - 1:HG_CHUNK, :]
            hg_dec[cc, hh] = jnp.exp(b_last)
            hg_qs[cc, hh] = _bf(proj_s[rows, cols] * jnp.exp(b))
            hg_kv[cc, hh] = _dot_tn(hg_v[hh][rows], _bf(key[rows, cols] * jnp.exp(b_last - b)))
        hg_in = {}
        for hh in range(HEADS):
            st_t = hg_state_s[hh]
            for cc in range(n_hg):
                hg_in[cc, hh] = _bf(st_t)
                st_t = st_t * hg_dec[cc, hh] + hg_kv[cc, hh]
            hg_state_s[hh] = st_t
        hg_out = {p: hg_intra[p[1]][hg_rows[p[0]]] + _dot_nt(hg_qs[p], hg_in[p]) for p in hg_pairs}
        for cc, hh in hg_pairs:
            rows = hg_rows[cc]
            o = hg_out[cc, hh]
            on = o * lax.rsqrt(jnp.mean(o * o, axis=-1, keepdims=True) + EPS) * hgw_ref[...]
            gh = proj_s[rows, 3 * hw + hh * hd:3 * hw + (hh + 1) * hd]
            mix_s[rows, head_cols[hh]] = _bf(on * _silu(gh))

        qkv_raw = proj_s[:, 4 * hw:7 * hw]
        qkv = _silu(_causal_conv(qkv_raw, cprev_s[...], cw_ref, None))
        cprev_s[...] = qkv_raw[ts - SUBLANES:ts, :]
        ab = proj_s[:, 8 * hw:8 * hw + LANES]
        g_all = -jnp.exp(alog_ref[...]) * _softplus(ab + dtb_ref[...])
        beta_all = jax.nn.sigmoid(ab)
        incl64, strict64 = _block_masks(ts, GD_CHUNK)
        gcum = _dot_exact_rhs(jnp.where(incl64, 1.0, 0.0).astype(BF16), g_all)
        gcum_t = gcum.T
        n_gd = ts // GD_CHUNK
        gd_q, gd_k, gd_gcol, gd_qk, a_mats, rhs_all = [], [], [], [], [], []
        for hh in range(HEADS):
            qh = qkv[:, hh * hd:(hh + 1) * hd]
            kh = qkv[:, hw + hh * hd:hw + (hh + 1) * hd]
            vh = qkv[:, 2 * hw + hh * hd:2 * hw + (hh + 1) * hd]
            qh = qh * lax.rsqrt(jnp.sum(qh * qh, axis=-1, keepdims=True) + EPS) * (hd ** -0.5)
            kh = kh * lax.rsqrt(jnp.sum(kh * kh, axis=-1, keepdims=True) + EPS)
            gcol = gcum[:, hh:hh + 1]
            grow = gcum_t[hh:hh + 1, :]
            beta = beta_all[:, HEADS + hh:HEADS + hh + 1]
            decay = jnp.where(incl64, jnp.exp(jnp.where(incl64, gcol - grow, 0.0)), 0.0)
            kb = kh * beta
            khb = _bf(kh)
            a_mats.append(jnp.where(strict64, _dot_nt(_bf(kb), khb) * decay, 0.0))
            gd_qk.append(_bf(jnp.where(incl64, _dot_nt(_bf(qh), khb) * decay, 0.0)))
            egc = jnp.exp(gcol)
            rhs_all.append(_bf(jnp.concatenate([vh * beta, kb * egc], axis=1)))
            gd_q.append(_bf(qh * egc))
            gd_k.append(kh)
            gd_gcol.append(gcol)
        t_inv = _unit_lower_inverses(a_mats, ts, GD_CHUNK, GD_BASE)
        sol = [_dot(_bf(t), rh) for t, rh in zip(t_inv, rhs_all)]
        chunk_rows = [slice(cc * GD_CHUNK, (cc + 1) * GD_CHUNK) for cc in range(n_gd)]
        gd_states = [[None] * n_gd for _ in range(HEADS)]
        ktu = [[None] * n_gd for _ in range(HEADS)]
        ktw = [[None] * n_gd for _ in range(HEADS)]
        g_last = [[None] * n_gd for _ in range(HEADS)]
        for cc, rows in enumerate(chunk_rows):
            for hh in range(HEADS):
                gcol = gd_gcol[hh]
                g_last[hh][cc] = gcol[(cc + 1) * GD_CHUNK - 1:(cc + 1) * GD_CHUNK, :]
                k_g = _bf(gd_k[hh][rows] * jnp.exp(g_last[hh][cc] - gcol[rows]))
                ktu[hh][cc] = _dot_tn(k_g, _bf(sol[hh][rows, :hd]))
                ktw[hh][cc] = _bf(_dot_tn(k_g, _bf(sol[hh][rows, hd:])))
        st = [gd_state_s[hh] for hh in range(HEADS)]
        for cc in range(n_gd):
            for hh in range(HEADS):
                gd_states[hh][cc] = _bf(st[hh])
                st[hh] = st[hh] * jnp.exp(g_last[hh][cc]) + ktu[hh][cc] - _dot(ktw[hh][cc], gd_states[hh][cc])
        for hh in range(HEADS):
            gd_state_s[hh] = st[hh]
        gd_pairs = [(cc, hh) for cc in range(n_gd) for hh in range(HEADS)]
        wq = {(cc, hh): _dot(jnp.concatenate([_bf(sol[hh][chunk_rows[cc], hd:]), gd_q[hh][chunk_rows[cc]]], axis=0),
                             gd_states[hh][cc]) for cc, hh in gd_pairs}
        v_newb = {(cc, hh): _bf(sol[hh][chunk_rows[cc], :hd] - wq[cc, hh][:GD_CHUNK]) for cc, hh in gd_pairs}
        gd_out = {(cc, hh): wq[cc, hh][GD_CHUNK:] + _dot(gd_qk[hh][chunk_rows[cc], chunk_rows[cc]], v_newb[cc, hh])
                  for cc, hh in gd_pairs}
        for cc, hh in gd_pairs:
            rows = chunk_rows[cc]
            o = gd_out[cc, hh]
            on = o * lax.rsqrt(jnp.mean(o * o, axis=-1, keepdims=True) + EPS) * gdw_ref[...]
            zh = proj_s[rows, 7 * hw + hh * hd:7 * hw + (hh + 1) * hd]
            mix_s[rows, hw + hh * hd:hw + (hh + 1) * hd] = _bf(on * _silu(zh))

    @pl.when(pl.program_id(1) == 0)
    def _():
        cprev_s[...] = jnp.zeros_like(cprev_s)
        hg_state_s[...] = jnp.zeros_like(hg_state_s)
        gd_state_s[...] = jnp.zeros_like(gd_state_s)

    x = x_ref[0]
    shift, scale, gate = mod_ref[0, 0:1, :], mod_ref[0, 1:2, :], mod_ref[0, 2:3, :]
    hb = _bf(_norm_modulate(x, nw_ref[...], shift, scale))
    proj_all[:, 0:8 * hw] = _dot(hb, win_ref[...])
    proj_all[:, 8 * hw:] = _dot(hb, wtail_ref[...])
    for t in range(n_sub):
        rows = pl.ds(t * ts, ts)
        mix_tile(proj_all.at[rows], mix_all.at[rows])
    o_ref[0] = x + gate * _dot(mix_all[...], wout_ref[...])


def _odd_mixer(x, mod, norm_w, lb_logits, layer, w_in, hg_norm_w, conv_w, a_log, dt_bias, gd_norm_w, w_out,
               ts=256, n_sub=4):
    bsz, seq, d = x.shape
    hw = HEADS * HEAD_DIM
    n_in = 8 * hw + LANES
    w_main = _bf(w_in[:, :8 * hw])
    w_tail = jnp.zeros((d, LANES), BF16).at[:, :w_in.shape[1] - 8 * hw].set(_bf(w_in[:, 8 * hw:]))
    lane_pad = lambda v, off: jnp.zeros((1, LANES), F32).at[0, off:off + v.shape[0]].set(v.astype(F32))
    row = lambda v: v.reshape(1, -1).astype(F32)
    depth = lb_logits.shape[0]
    rows = n_sub * ts
    kern = functools.partial(_odd_kernel, ts=ts, layer=layer)
    return pl.pallas_call(
        kern,
        grid=(bsz, seq // rows),
        in_specs=[pl.BlockSpec((1, rows, d), lambda b, j: (b, j, 0)),
                  pl.BlockSpec((1, 6, d), lambda b, j: (b, 0, 0)),
                  _const_spec((1, d)), _const_spec((d, 8 * hw)), _const_spec((d, LANES)), _const_spec((depth, hw)),
                  _const_spec((1, HEAD_DIM)), _const_spec((CONV_WIDTH, 3 * hw)),
                  _const_spec((1, LANES)), _const_spec((1, LANES)), _const_spec((1, HEAD_DIM)),
                  _const_spec((2 * hw, d))],
        out_specs=pl.BlockSpec((1, rows, d), lambda b, j: (b, j, 0)),
        out_shape=jax.ShapeDtypeStruct(x.shape, F32),
        scratch_shapes=[pltpu.VMEM((rows, n_in), F32), pltpu.VMEM((ts, hw), F32),
                        pltpu.VMEM((ts, hw), BF16), pltpu.VMEM((ts, hw), BF16),
                        pltpu.VMEM((rows, 2 * hw), BF16), pltpu.VMEM((SUBLANES, 3 * hw), F32),
                        pltpu.VMEM((HEADS, HEAD_DIM, HEAD_DIM), F32), pltpu.VMEM((HEADS, HEAD_DIM, HEAD_DIM), F32)],
        compiler_params=pltpu.CompilerParams(
            dimension_semantics=("arbitrary", "arbitrary"), vmem_limit_bytes=VMEM_LIMIT_BYTES),
        name="odd_mixer",
    )(x, mod, row(norm_w), w_main, w_tail, lb_logits.astype(F32), row(hg_norm_w), conv_w.astype(F32),
      lane_pad(a_log, 0), lane_pad(dt_bias, 0), row(gd_norm_w), _bf(w_out))


def _mlp_kernel(x_ref, mod_ref, nw_ref, w1_ref, w2_ref, fw_ref, o_ref, *, n_split, final):
    x = x_ref[0]
    shift, scale, gate = mod_ref[0, 3:4, :], mod_ref[0, 4:5, :], mod_ref[0, 5:6, :]
    h = _bf(_norm_modulate(x, nw_ref[...], shift, scale))
    hidden = w1_ref.shape[2]
    step = hidden // n_split
    acc = jnp.zeros(x.shape, F32)
    for s in range(n_split):
        hid = jnp.square(jnp.maximum(_dot(h, w1_ref[0, :, s * step:(s + 1) * step]), 0.0))
        acc = acc + _dot(_bf(hid), w2_ref[0, s * step:(s + 1) * step, :])
    y = x + gate * acc
    if final:
        y = _rmsnorm(y, fw_ref[...])
    o_ref[0] = y


def _mlp(x, mod, norm_w, w1_all, w2_all, layer, final_w, final, tm=1024, n_split=4):
    bsz, seq, d = x.shape
    hidden = w1_all.shape[2]
    kern = functools.partial(_mlp_kernel, n_split=n_split, final=final)
    row = lambda v: v.reshape(1, -1).astype(F32)
    layer_block = lambda shape: pl.BlockSpec(shape, lambda b, j: (layer, 0, 0), pipeline_mode=pl.Buffered(1))
    return pl.pallas_call(
        kern,
        grid=(bsz, seq // tm),
        in_specs=[pl.BlockSpec((1, tm, d), lambda b, j: (b, j, 0)),
                  pl.BlockSpec((1, 6, d), lambda b, j: (b, 0, 0)),
                  _const_spec((1, d)), layer_block((1, d, hidden)), layer_block((1, hidden, d)),
                  _const_spec((1, d))],
        out_specs=pl.BlockSpec((1, tm, d), lambda b, j: (b, j, 0)),
        out_shape=jax.ShapeDtypeStruct(x.shape, F32),
        compiler_params=pltpu.CompilerParams(
            dimension_semantics=("arbitrary", "arbitrary"), vmem_limit_bytes=VMEM_LIMIT_BYTES),
        name="mlp_final" if final else "mlp",
    )(x, mod, row(norm_w), w1_all, w2_all, row(final_w))


def kernel(x, c, positions, ada_w, ada_b, norm_mix_w, norm_mlp_w, mlp_w1, mlp_w2, final_norm_w, ev_w_in, lru_conv_w, lru_conv_b, lru_w_a, lru_b_a, lru_w_x, lru_b_x, lru_lambda, ev_w_out, hg_lb_logits, od_w_in, hg_norm_w, gd_conv_w, gd_a_log, gd_dt_bias, gd_norm_w, od_w_out):
    depth = ada_w.shape[0]
    bsz, seq, d = x.shape
    mod = _adaln_mod(c, ada_w, ada_b).reshape(depth, bsz, 6, d)
    w1_all, w2_all = _bf(mlp_w1), _bf(mlp_w2)
    for layer in range(depth):
        j = layer // 2
        if layer % 2 == 0:
            x = _even_mixer(x, mod[layer], positions, norm_mix_w[layer], ev_w_in[j], lru_conv_w[j], lru_conv_b[j],
                            lru_w_a[j], lru_b_a[j], lru_w_x[j], lru_b_x[j], lru_lambda[j], ev_w_out[j])
        else:
            x = _odd_mixer(x, mod[layer], norm_mix_w[layer], hg_lb_logits, layer, od_w_in[j], hg_norm_w[j],
                           gd_conv_w[j], gd_a_log[j], gd_dt_bias[j], gd_norm_w[j], od_w_out[j])
        x = _mlp(x, mod[layer], norm_mlp_w[layer], w1_all, w2_all, layer, final_norm_w,
                 final=(layer == depth - 1))
    return x
```

```python
import functools
import math

import jax
import jax.numpy as jnp
from jax import lax
from jax.experimental import pallas as pl
from jax.experimental.pallas import tpu as pltpu

F32 = jnp.float32
BF16 = jnp.bfloat16

EPS = 1e-6
CONV_WIDTH = 4
LRU_C = 8.0
ROPE_BASE = 10000.0
HEADS = 4
HEAD_DIM = 128
HG_CHUNK = 32
GD_CHUNK = 64
GD_BASE = 16
SUBLANES = 8
LANES = 128
VMEM_LIMIT_BYTES = 56 * 1024 * 1024


def _dot(a, b):
    return jnp.dot(a, b, preferred_element_type=F32)


def _dot_nt(a, b):
    return lax.dot_general(a, b, (((1,), (1,)), ((), ())), preferred_element_type=F32)


def _dot_tn(a, b):
    return lax.dot_general(a, b, (((0,), (0,)), ((), ())), preferred_element_type=F32)


def _bf(x):
    return x.astype(BF16)


def _dot_exact_rhs(sel_bf16, x):
    hi = _bf(x)
    lo = _bf(x - hi.astype(F32))
    return _dot(sel_bf16, hi) + _dot(sel_bf16, lo)


def _softplus(x):
    return jnp.maximum(x, 0.0) + jnp.log1p(jnp.exp(-jnp.abs(x)))


def _neg_expm1(x):
    t = jnp.tanh(0.5 * x)
    return -2.0 * t / (1.0 - t)


def _silu(x):
    return x * jax.nn.sigmoid(x)


def _rmsnorm(x, w):
    return x * lax.rsqrt(jnp.mean(x * x, axis=-1, keepdims=True) + EPS) * w


def _norm_modulate(x, w, shift, scale):
    return x * lax.rsqrt(jnp.mean(x * x, axis=-1, keepdims=True) + EPS) * (w * (1.0 + scale)) + shift


def _causal_conv(x, prev8, w_ref, bias):
    n, c = x.shape
    x_prev = jnp.concatenate([prev8, x[:n - SUBLANES]], axis=0)
    sub = lax.broadcasted_iota(jnp.int32, (n, c), 0) & (SUBLANES - 1)
    acc = x * w_ref[CONV_WIDTH - 1:CONV_WIDTH, :]
    if bias is not None:
        acc = acc + bias
    for s in range(1, CONV_WIDTH):
        z = jnp.where(sub >= SUBLANES - s, x_prev, x)
        delayed = pltpu.roll(z.reshape(n // SUBLANES, SUBLANES, c), s, axis=1).reshape(n, c)
        acc = acc + delayed * w_ref[CONV_WIDTH - 1 - s:CONV_WIDTH - s, :]
    return acc


def _same_block(n, blk):
    r = lax.broadcasted_iota(jnp.int32, (n, n), 0)
    c = lax.broadcasted_iota(jnp.int32, (n, n), 1)
    sh = int(math.log2(blk))
    return lax.shift_right_logical(r, sh) == lax.shift_right_logical(c, sh)


def _block_masks(n, blk):
    r = lax.broadcasted_iota(jnp.int32, (n, n), 0)
    c = lax.broadcasted_iota(jnp.int32, (n, n), 1)
    same = _same_block(n, blk)
    return same & (c <= r), same & (c < r)


def _unit_lower_inverses(a_mats, n, blk, base):
    r = lax.broadcasted_iota(jnp.int32, (n, n), 0)
    c = lax.broadcasted_iota(jnp.int32, (n, n), 1)
    inner = _same_block(n, base)
    pw = [jnp.where(inner, a, 0.0) for a in a_mats]
    t_inv = [jnp.where(r == c, 1.0, 0.0) - p for p in pw]
    pwb = [_bf(p) for p in pw]
    pw = [_dot(p, p) for p in pwb]
    for _ in range(int(math.log2(base)) - 2):
        pwb = [_bf(p) for p in pw]
        both = [_dot(jnp.concatenate([_bf(t), p], axis=0), p) for t, p in zip(t_inv, pwb)]
        t_inv = [t + b[:n] for t, b in zip(t_inv, both)]
        pw = [b[n:] for b in both]
    t_inv = [t + _dot(_bf(t), _bf(p)) for t, p in zip(t_inv, pw)]
    size = base
    while size < blk:
        size *= 2
        outer = _same_block(n, size)
        off = [_bf(jnp.where(outer & jnp.logical_not(inner), a, 0.0)) for a in a_mats]
        tb = [_bf(t) for t in t_inv]
        t_inv = [t - _dot(b, _bf(_dot(o, b))) for t, b, o in zip(t_inv, tb, off)]
        inner = outer
    return t_inv


def _mod_kernel(c_ref, w_ref, b_ref, o_ref):
    c = c_ref[...]
    o_ref[0] = _dot(_bf(_silu(c)), _bf(w_ref[0])) + b_ref[0]


def _adaln_mod(c, ada_w, ada_b, tn=1536):
    depth, d, n = ada_w.shape
    b = c.shape[0]
    return pl.pallas_call(
        _mod_kernel,
        grid=(depth, n // tn),
        in_specs=[pl.BlockSpec((b, d), lambda l, j: (0, 0)),
                  pl.BlockSpec((1, d, tn), lambda l, j: (l, 0, j)),
                  pl.BlockSpec((1, 1, tn), lambda l, j: (l, 0, j))],
        out_specs=pl.BlockSpec((1, b, tn), lambda l, j: (l, 0, j)),
        out_shape=jax.ShapeDtypeStruct((depth, b, n), F32),
        compiler_params=pltpu.CompilerParams(
            dimension_semantics=("arbitrary", "arbitrary"), vmem_limit_bytes=VMEM_LIMIT_BYTES),
        name="adaln_mod",
    )(c, ada_w, ada_b.reshape(depth, 1, n))


def _lru_scan(a_s, u_s, h0, ts):
    width = a_s.shape[1]
    row = lax.broadcasted_iota(jnp.int32, (SUBLANES, width), 0)

    hprev = h0
    for g in range(ts // SUBLANES):
        rows = slice(g * SUBLANES, (g + 1) * SUBLANES)
        a8 = a_s[rows, :]
        u8 = u_s[rows, :]
        for d in (1, 2, 4):
            keep = row >= d
            a_sh = jnp.where(keep, pltpu.roll(a8, d, axis=0), 1.0)
            u_sh = jnp.where(keep, pltpu.roll(u8, d, axis=0), 0.0)
            u8 = u8 + a8 * u_sh
            a8 = a8 * a_sh
        h8 = u8 + a8 * hprev
        u_s[rows, :] = h8
        hprev = h8[SUBLANES - 1:SUBLANES, :]
    return hprev


def _even_kernel(x_ref, mod_ref, pos_ref, coff_ref, soff_ref, inv_ref, sgn_ref, nw_ref, win_ref, cw_ref,
                 cb_ref, wga_ref, wgx_ref, ba_ref, bx_ref, lam_ref, wout_ref, o_ref,
                 proj_all, a_s, u_s, mix_all, cos_s, sin_s, xprev_s, hprev_s, state_s, *, ts, chunk):
    lw = a_s.shape[1]
    half = lw // 2
    hd = HEAD_DIM
    n_sub = proj_all.shape[0] // ts

    def mix_tile(proj_s, mix_s, cos_view, sin_view):
        xr = proj_s[:, 0:lw]
        xc = _causal_conv(xr, xprev_s[...], cw_ref, cb_ref[...])
        xprev_s[...] = xr[ts - SUBLANES:ts, :]
        xcb = _bf(xc)
        r_pre = jnp.concatenate([_dot(xcb[:, :half], wga_ref[0]), _dot(xcb[:, half:], wga_ref[1])], axis=1)
        i_pre = jnp.concatenate([_dot(xcb[:, :half], wgx_ref[0]), _dot(xcb[:, half:], wgx_ref[1])], axis=1)
        r = jax.nn.sigmoid(r_pre + ba_ref[...])
        i = jax.nn.sigmoid(i_pre + bx_ref[...])
        log_a = (-LRU_C) * r * _softplus(-lam_ref[...])
        a_s[...] = jnp.exp(log_a)
        u_s[...] = jnp.sqrt(_neg_expm1(2.0 * log_a)) * (i * xc)
        hprev_s[...] = _lru_scan(a_s, u_s, hprev_s[...], ts)
        yr = proj_s[:, lw:2 * lw]
        mix_s[:, 0:lw] = _bf(u_s[...] * jax.nn.gelu(yr, approximate=True))

        cos_p, sin_p = cos_view[...], sin_view[...]
        lane = lax.broadcasted_iota(jnp.int32, (ts // 2, hd), 1)
        first = lane < hd // 2
        cos_r, sin_r = pltpu.roll(cos_p, hd // 2, axis=1), pltpu.roll(sin_p, hd // 2, axis=1)
        cos_d = jnp.concatenate([jnp.where(first, cos_p, cos_r), jnp.where(first, cos_r, cos_p)], axis=0)
        sin_d = jnp.concatenate([jnp.where(first, sin_p, sin_r), jnp.where(first, sin_r, sin_p)], axis=0)
        sin_d = sin_d * sgn_ref[...]
        ridx = lax.broadcasted_iota(jnp.int32, (chunk, chunk), 0)
        cidx = lax.broadcasted_iota(jnp.int32, (chunk, chunk), 1)
        rel = (ridx - cidx).astype(F32)
        tcol = lax.broadcasted_iota(jnp.int32, (chunk, 1), 0).astype(F32)
        qoff, koff, voff, goff = 2 * lw, 2 * lw + HEADS * hd, 2 * lw + 2 * HEADS * hd, 2 * lw + 3 * HEADS * hd
        log_gamma = [math.log1p(-(2.0 ** (-5.0 - hh))) for hh in range(HEADS)]
        dmats = [jnp.where(rel >= 0, jnp.exp(lg * jnp.maximum(rel, 0.0)), 0.0) for lg in log_gamma]
        q_decs = [jnp.exp(lg * (tcol + 1.0)) for lg in log_gamma]
        k_decs = [jnp.exp(lg * (chunk - 1.0 - tcol)) for lg in log_gamma]
        n_ret = ts // chunk
        pairs = [(cc, hh) for cc in range(n_ret) for hh in range(HEADS)]
        qb, vb, sc, kv = {}, {}, {}, {}
        for cc, hh in pairs:
            rows = slice(cc * chunk, (cc + 1) * chunk)
            cos_c, sin_c = cos_d[rows], sin_d[rows]
            qc = proj_s[rows, qoff + hh * hd:qoff + (hh + 1) * hd]
            kc = proj_s[rows, koff + hh * hd:koff + (hh + 1) * hd]
            qc = qc * cos_c + pltpu.roll(qc, hd // 2, axis=1) * sin_c
            kc = (kc * cos_c + pltpu.roll(kc, hd // 2, axis=1) * sin_c) * (hd ** -0.5)
            vb[cc, hh] = _bf(proj_s[rows, voff + hh * hd:voff + (hh + 1) * hd])
            qb[cc, hh] = _bf(qc)
            sc[cc, hh] = _bf(_dot_nt(qb[cc, hh], _bf(kc)) * dmats[hh])
            kv[cc, hh] = _dot_tn(_bf(kc * k_decs[hh]), vb[cc, hh])
        st_in = {}
        for hh, lg in enumerate(log_gamma):
            st = state_s[hh]
            for cc in range(n_ret):
                st_in[cc, hh] = _bf(st)
                st = math.exp(lg * chunk) * st + kv[cc, hh]
            state_s[hh] = st
        outs = {p: _dot(sc[p], vb[p]) + _dot(qb[p], st_in[p]) * q_decs[p[1]] for p in pairs}
        for cc, hh in pairs:
            rows = slice(cc * chunk, (cc + 1) * chunk)
            o = outs[cc, hh]
            mu = jnp.mean(o, axis=-1, keepdims=True)
            oc = o - mu
            var = jnp.mean(oc * oc, axis=-1, keepdims=True)
            g = proj_s[rows, goff + hh * hd:goff + (hh + 1) * hd]
            mix_s[rows, lw + hh * hd:lw + (hh + 1) * hd] = _bf(oc * lax.rsqrt(var + EPS) * _silu(g))

    @pl.when(pl.program_id(1) == 0)
    def _():
        xprev_s[...] = jnp.zeros_like(xprev_s)
        hprev_s[...] = jnp.zeros_like(hprev_s)
        state_s[...] = jnp.zeros_like(state_s)

    pos = pos_ref[0]
    base = pos[0:1, 0:1]
    row_i = lax.broadcasted_iota(jnp.int32, pos.shape, 0)
    lane_i = lax.broadcasted_iota(jnp.int32, pos.shape, 1)
    dev = jnp.abs(pos - base - (row_i * LANES + lane_i).astype(F32))
    dev = jnp.max(jnp.max(dev, axis=1, keepdims=True), axis=0, keepdims=True)
    consecutive = dev[0, 0] == 0.0

    @pl.when(consecutive)
    def _():
        ang0 = base * inv_ref[...]
        c0, s0 = jnp.cos(ang0), jnp.sin(ang0)
        cos_s[...] = c0 * coff_ref[...] - s0 * soff_ref[...]
        sin_s[...] = s0 * coff_ref[...] + c0 * soff_ref[...]

    @pl.when(jnp.logical_not(consecutive))
    def _():
        pos_t = pos.T
        first_half = lax.broadcasted_iota(jnp.int32, (ts // 2, hd), 1) < hd // 2
        for t in range(n_sub):
            packed = pl.ds(t * (ts // 2), ts // 2)
            ang = jnp.where(first_half, pos_t[:, 2 * t:2 * t + 1], pos_t[:, 2 * t + 1:2 * t + 2]) * inv_ref[...]
            cos_s[packed, :] = jnp.cos(ang)
            sin_s[packed, :] = jnp.sin(ang)

    x = x_ref[0]
    shift, scale, gate = mod_ref[0, 0:1, :], mod_ref[0, 1:2, :], mod_ref[0, 2:3, :]
    proj_all[...] = _dot(_bf(_norm_modulate(x, nw_ref[...], shift, scale)), win_ref[...])
    for t in range(n_sub):
        rows = pl.ds(t * ts, ts)
        packed = pl.ds(t * (ts // 2), ts // 2)
        mix_tile(proj_all.at[rows], mix_all.at[rows], cos_s.at[packed], sin_s.at[packed])
    o_ref[0] = x + gate * _dot(mix_all[...], wout_ref[...])


def _blockdiag_halves(w):
    n, bi, bj = w.shape
    eye = jnp.eye(n // 2, dtype=w.dtype)
    halves = []
    for p in range(2):
        blocks = w[p * (n // 2):(p + 1) * (n // 2)]
        dense = jnp.einsum('nm,nij->nimj', eye, blocks).reshape(n // 2 * bi, n // 2 * bj)
        halves.append(dense)
    return jnp.stack(halves)


def _const_spec(shape):
    nd = len(shape)
    return pl.BlockSpec(shape, lambda *_: (0,) * nd, pipeline_mode=pl.Buffered(1))


def _pack_half_tiles(v, ts):
    lead = v.shape[:-1]
    p = v.reshape(lead + (v.shape[-1] // ts, 2, ts // 2))
    p = jnp.swapaxes(p, -1, -2)
    p = jnp.repeat(p, HEAD_DIM // 2, axis=-1)
    return p.reshape(lead + (v.shape[-1] // 2, HEAD_DIM))


def _even_mixer(x, mod, positions, norm_w, w_in, conv_w, conv_b, w_a, b_a, w_x, b_x, lam, w_out,
                ts=256, n_sub=4, chunk=128):
    bsz, seq, d = x.shape
    lw = lam.shape[-1]
    n_in = w_in.shape[1]
    half = HEAD_DIM // 2
    inv = ROPE_BASE ** (-jnp.arange(half, dtype=F32) / half)
    inv2 = jnp.concatenate([inv, inv]).reshape(1, HEAD_DIM)
    sgn = jnp.concatenate([-jnp.ones((half,), F32), jnp.ones((half,), F32)]).reshape(1, HEAD_DIM)
    row = lambda v: v.reshape(1, -1).astype(F32)
    rows = n_sub * ts
    assert ts == 2 * LANES, "the packed rotary layout pairs the two 128-row halves of a tile"
    pos_rows = positions.astype(F32).reshape(bsz, seq // LANES, LANES)
    off_packed = _pack_half_tiles(jnp.arange(rows, dtype=F32), ts)
    cos_off, sin_off = jnp.cos(off_packed * inv2), jnp.sin(off_packed * inv2)
    kern = functools.partial(_even_kernel, ts=ts, chunk=chunk)
    return pl.pallas_call(
        kern,
        grid=(bsz, seq // rows),
        in_specs=[pl.BlockSpec((1, rows, d), lambda b, j: (b, j, 0)),
                  pl.BlockSpec((1, 6, d), lambda b, j: (b, 0, 0)),
                  pl.BlockSpec((1, rows // LANES, LANES), lambda b, j: (b, j, 0)),
                  _const_spec((rows // 2, HEAD_DIM)), _const_spec((rows // 2, HEAD_DIM)),
                  _const_spec((1, HEAD_DIM)), _const_spec((1, HEAD_DIM)), _const_spec((1, d)),
                  _const_spec((d, n_in)), _const_spec((CONV_WIDTH, lw)), _const_spec((1, lw)),
                  _const_spec((2, lw // 2, lw // 2)), _const_spec((2, lw // 2, lw // 2)),
                  _const_spec((1, lw)), _const_spec((1, lw)), _const_spec((1, lw)),
                  _const_spec((lw + HEADS * HEAD_DIM, d))],
        out_specs=pl.BlockSpec((1, rows, d), lambda b, j: (b, j, 0)),
        out_shape=jax.ShapeDtypeStruct(x.shape, F32),
        scratch_shapes=[pltpu.VMEM((rows, n_in), F32), pltpu.VMEM((ts, lw), F32), pltpu.VMEM((ts, lw), F32),
                        pltpu.VMEM((rows, lw + HEADS * HEAD_DIM), BF16),
                        pltpu.VMEM((rows // 2, HEAD_DIM), F32), pltpu.VMEM((rows // 2, HEAD_DIM), F32),
                        pltpu.VMEM((SUBLANES, lw), F32),
                        pltpu.VMEM((1, lw), F32), pltpu.VMEM((HEADS, HEAD_DIM, HEAD_DIM), F32)],
        compiler_params=pltpu.CompilerParams(
            dimension_semantics=("arbitrary", "arbitrary"), vmem_limit_bytes=VMEM_LIMIT_BYTES),
        name="even_mixer",
    )(x, mod, pos_rows, cos_off, sin_off, inv2, sgn, row(norm_w), _bf(w_in), conv_w.astype(F32), row(conv_b),
      _bf(_blockdiag_halves(w_a)), _bf(_blockdiag_halves(w_x)), row(b_a), row(b_x), row(lam), _bf(w_out))


def _odd_kernel(x_ref, mod_ref, nw_ref, win_ref, wtail_ref, lbl_ref, hgw_ref, cw_ref, alog_ref, dtb_ref, gdw_ref,
                wout_ref, o_ref, proj_all, b_s, qi_s, ki_s, mix_all, cprev_s, hg_state_s, gd_state_s,
                *, ts, layer):
    hd = HEAD_DIM
    hw = HEADS * hd
    n_sub = proj_all.shape[0] // ts

    def mix_tile(proj_s, mix_s):
        logits = lbl_ref[...]
        e = jnp.exp(logits - jnp.max(logits, axis=0, keepdims=True))
        lb = jnp.sum(e[1:layer + 1], axis=0, keepdims=True) / jnp.sum(e, axis=0, keepdims=True)
        key = (1.0 - lb) * jax.nn.sigmoid(-proj_s[:, hw:2 * hw])
        log_f = jnp.log1p(-key)
        incl32, _ = _block_masks(ts, HG_CHUNK)
        b_s[...] = _dot_exact_rhs(jnp.where(incl32, 1.0, 0.0).astype(BF16), log_f)
        mid = HG_CHUNK // 2
        n_hg = ts // HG_CHUNK
        for cc in range(n_hg):
            rows = slice(cc * HG_CHUNK, (cc + 1) * HG_CHUNK)
            b = b_s[rows, :]
            b_mid = b[mid:mid + 1, :]
            qi_s[rows, :] = _bf(proj_s[rows, 0:hw] * jnp.exp(b - b_mid))
            ki_s[rows, :] = _bf(key[rows] * jnp.exp(b_mid - b))
        head_cols = [slice(hh * hd, (hh + 1) * hd) for hh in range(HEADS)]
        hg_v = [_bf(proj_s[:, 2 * hw + hh * hd:2 * hw + (hh + 1) * hd]) for hh in range(HEADS)]
        hg_scores = [jnp.where(incl32, _dot_nt(qi_s[:, cols], ki_s[:, cols]), 0.0) for cols in head_cols]
        hg_intra = [_dot(_bf(sc), v) for sc, v in zip(hg_scores, hg_v)]
        hg_pairs = [(cc, hh) for cc in range(n_hg) for hh in range(HEADS)]
        hg_rows = [slice(cc * HG_CHUNK, (cc + 1) * HG_CHUNK) for cc in range(n_hg)]
        hg_kv, hg_dec, hg_qs = {}, {}, {}
        for cc, hh in hg_pairs:
            rows, cols = hg_rows[cc], head_cols[hh]
            b = b_s[rows, cols]
            b_last = b[HG_CHUNK - 1:HG_CHUNK, :]
            hg_dec[cc, hh] = jnp.exp(b_last)
            hg_qs[cc, hh] = _bf(proj_s[rows, cols] * jnp.exp(b))
            hg_kv[cc, hh] = _dot_tn(hg_v[hh][rows], _bf(key[rows, cols] * jnp.exp(b_last - b)))
        hg_in = {}
        for hh in range(HEADS):
            st_t = hg_state_s[hh]
            for cc in range(n_hg):
                hg_in[cc, hh] = _bf(st_t)
                st_t = st_t * hg_dec[cc, hh] + hg_kv[cc, hh]
            hg_state_s[hh] = st_t
        hg_out = {p: hg_intra[p[1]][hg_rows[p[0]]] + _dot_nt(hg_qs[p], hg_in[p]) for p in hg_pairs}
        for cc, hh in hg_pairs:
            rows = hg_rows[cc]
            o = hg_out[cc, hh]
            on = o * lax.rsqrt(jnp.mean(o * o, axis=-1, keepdims=True) + EPS) * hgw_ref[...]
            gh = proj_s[rows, 3 * hw + hh * hd:3 * hw + (hh + 1) * hd]
            mix_s[rows, head_cols[hh]] = _bf(on * _silu(gh))

        qkv_raw = proj_s[:, 4 * hw:7 * hw]
        qkv = _silu(_causal_conv(qkv_raw, cprev_s[...], cw_ref, None))
        cprev_s[...] = qkv_raw[ts - SUBLANES:ts, :]
        ab = proj_s[:, 8 * hw:8 * hw + LANES]
        g_all = -jnp.exp(alog_ref[...]) * _softplus(ab + dtb_ref[...])
        beta_all = jax.nn.sigmoid(ab)
        incl64, strict64 = _block_masks(ts, GD_CHUNK)
        gcum = _dot_exact_rhs(jnp.where(incl64, 1.0, 0.0).astype(BF16), g_all)
        gcum_t = gcum.T
        n_gd = ts // GD_CHUNK
        gd_q, gd_k, gd_gcol, gd_qk, a_mats, rhs_all = [], [], [], [], [], []
        for hh in range(HEADS):
            qh = qkv[:, hh * hd:(hh + 1) * hd]
            kh = qkv[:, hw + hh * hd:hw + (hh + 1) * hd]
            vh = qkv[:, 2 * hw + hh * hd:2 * hw + (hh + 1) * hd]
            qh = qh * lax.rsqrt(jnp.sum(qh * qh, axis=-1, keepdims=True) + EPS) * (hd ** -0.5)
            kh = kh * lax.rsqrt(jnp.sum(kh * kh, axis=-1, keepdims=True) + EPS)
            gcol = gcum[:, hh:hh + 1]
            grow = gcum_t[hh:hh + 1, :]
            beta = beta_all[:, HEADS + hh:HEADS + hh + 1]
            decay = jnp.where(incl64, jnp.exp(jnp.where(incl64, gcol - grow, 0.0)), 0.0)
            kb = kh * beta
            khb = _bf(kh)
            kq = _dot_nt(jnp.concatenate([_bf(kb), _bf(qh)], axis=0), khb)
            a_mats.append(jnp.where(strict64, kq[:ts] * decay, 0.0))
            gd_qk.append(_bf(jnp.where(incl64, kq[ts:] * decay, 0.0)))
            egc = jnp.exp(gcol)
            rhs_all.append(_bf(jnp.concatenate([vh * beta, kb * egc], axis=1)))
            gd_q.append(_bf(qh * egc))
            gd_k.append(kh)
            gd_gcol.append(gcol)
        t_inv = _unit_lower_inverses(a_mats, ts, GD_CHUNK, GD_BASE)
        sol = [_dot(_bf(t), rh) for t, rh in zip(t_inv, rhs_all)]
        chunk_rows = [slice(cc * GD_CHUNK, (cc + 1) * GD_CHUNK) for cc in range(n_gd)]
        gd_states = [[None] * n_gd for _ in range(HEADS)]
        ktu = [[None] * n_gd for _ in range(HEADS)]
        ktw = [[None] * n_gd for _ in range(HEADS)]
        g_last = [[None] * n_gd for _ in range(HEADS)]
        for cc, rows in enumerate(chunk_rows):
            for hh in range(HEADS):
                gcol = gd_gcol[hh]
                g_last[hh][cc] = gcol[(cc + 1) * GD_CHUNK - 1:(cc + 1) * GD_CHUNK, :]
                k_g = _bf(gd_k[hh][rows] * jnp.exp(g_last[hh][cc] - gcol[rows]))
                kt = _dot_tn(k_g, _bf(sol[hh][rows]))
                ktu[hh][cc] = kt[:, :hd]
                ktw[hh][cc] = _bf(kt[:, hd:])
        st = [gd_state_s[hh] for hh in range(HEADS)]
        for cc in range(n_gd):
            for hh in range(HEADS):
                gd_states[hh][cc] = _bf(st[hh])
                st[hh] = st[hh] * jnp.exp(g_last[hh][cc]) + ktu[hh][cc] - _dot(ktw[hh][cc], gd_states[hh][cc])
        for hh in range(HEADS):
            gd_state_s[hh] = st[hh]
        gd_pairs = [(cc, hh) for cc in range(n_gd) for hh in range(HEADS)]
        wq = {(cc, hh): _dot(jnp.concatenate([_bf(sol[hh][chunk_rows[cc], hd:]), gd_q[hh][chunk_rows[cc]]], axis=0),
                             gd_states[hh][cc]) for cc, hh in gd_pairs}
        v_newb = {(cc, hh): _bf(sol[hh][chunk_rows[cc], :hd] - wq[cc, hh][:GD_CHUNK]) for cc, hh in gd_pairs}
        gd_out = {(cc, hh): wq[cc, hh][GD_CHUNK:] + _dot(gd_qk[hh][chunk_rows[cc], chunk_rows[cc]], v_newb[cc, hh])
                  for cc, hh in gd_pairs}
        for cc, hh in gd_pairs:
            rows = chunk_rows[cc]
            o = gd_out[cc, hh]
            on = o * lax.rsqrt(jnp.mean(o * o, axis=-1, keepdims=True) + EPS) * gdw_ref[...]
            zh = proj_s[rows, 7 * hw + hh * hd:7 * hw + (hh + 1) * hd]
            mix_s[rows, hw + hh * hd:hw + (hh + 1) * hd] = _bf(on * _silu(zh))

    @pl.when(pl.program_id(1) == 0)
    def _():
        cprev_s[...] = jnp.zeros_like(cprev_s)
        hg_state_s[...] = jnp.zeros_like(hg_state_s)
        gd_state_s[...] = jnp.zeros_like(gd_state_s)

    x = x_ref[0]
    shift, scale, gate = mod_ref[0, 0:1, :], mod_ref[0, 1:2, :], mod_ref[0, 2:3, :]
    hb = _bf(_norm_modulate(x, nw_ref[...], shift, scale))
    proj_all[:, 0:8 * hw] = _dot(hb, win_ref[...])
    proj_all[:, 8 * hw:] = _dot(hb, wtail_ref[...])
    for t in range(n_sub):
        rows = pl.ds(t * ts, ts)
        mix_tile(proj_all.at[rows], mix_all.at[rows])
    o_ref[0] = x + gate * _dot(mix_all[...], wout_ref[...])


def _odd_mixer(x, mod, norm_w, lb_logits, layer, w_in, hg_norm_w, conv_w, a_log, dt_bias, gd_norm_w, w_out,
               ts=256, n_sub=4):
    bsz, seq, d = x.shape
    hw = HEADS * HEAD_DIM
    n_in = 8 * hw + LANES
    w_main = _bf(w_in[:, :8 * hw])
    w_tail = jnp.zeros((d, LANES), BF16).at[:, :w_in.shape[1] - 8 * hw].set(_bf(w_in[:, 8 * hw:]))
    lane_pad = lambda v, off: jnp.zeros((1, LANES), F32).at[0, off:off + v.shape[0]].set(v.astype(F32))
    row = lambda v: v.reshape(1, -1).astype(F32)
    depth = lb_logits.shape[0]
    rows = n_sub * ts
    kern = functools.partial(_odd_kernel, ts=ts, layer=layer)
    return pl.pallas_call(
        kern,
        grid=(bsz, seq // rows),
        in_specs=[pl.BlockSpec((1, rows, d), lambda b, j: (b, j, 0)),
                  pl.BlockSpec((1, 6, d), lambda b, j: (b, 0, 0)),
                  _const_spec((1, d)), _const_spec((d, 8 * hw)), _const_spec((d, LANES)), _const_spec((depth, hw)),
                  _const_spec((1, HEAD_DIM)), _const_spec((CONV_WIDTH, 3 * hw)),
                  _const_spec((1, LANES)), _const_spec((1, LANES)), _const_spec((1, HEAD_DIM)),
                  _const_spec((2 * hw, d))],
        out_specs=pl.BlockSpec((1, rows, d), lambda b, j: (b, j, 0)),
        out_shape=jax.ShapeDtypeStruct(x.shape, F32),
        scratch_shapes=[pltpu.VMEM((rows, n_in), F32), pltpu.VMEM((ts, hw), F32),
                        pltpu.VMEM((ts, hw), BF16), pltpu.VMEM((ts, hw), BF16),
                        pltpu.VMEM((rows, 2 * hw), BF16), pltpu.VMEM((SUBLANES, 3 * hw), F32),
                        pltpu.VMEM((HEADS, HEAD_DIM, HEAD_DIM), F32), pltpu.VMEM((HEADS, HEAD_DIM, HEAD_DIM), F32)],
        compiler_params=pltpu.CompilerParams(
            dimension_semantics=("arbitrary", "arbitrary"), vmem_limit_bytes=VMEM_LIMIT_BYTES),
        name="odd_mixer",
    )(x, mod, row(norm_w), w_main, w_tail, lb_logits.astype(F32), row(hg_norm_w), conv_w.astype(F32),
      lane_pad(a_log, 0), lane_pad(dt_bias, 0), row(gd_norm_w), _bf(w_out))


def _mlp_kernel(x_ref, mod_ref, nw_ref, w1_ref, w2_ref, fw_ref, o_ref, *, n_split, final):
    x = x_ref[0]
    shift, scale, gate = mod_ref[0, 3:4, :], mod_ref[0, 4:5, :], mod_ref[0, 5:6, :]
    h = _bf(_norm_modulate(x, nw_ref[...], shift, scale))
    hidden = w1_ref.shape[2]
    step = hidden // n_split
    acc = jnp.zeros(x.shape, F32)
    for s in range(n_split):
        hid = jnp.square(jnp.maximum(_dot(h, w1_ref[0, :, s * step:(s + 1) * step]), 0.0))
        acc = acc + _dot(_bf(hid), w2_ref[0, s * step:(s + 1) * step, :])
    y = x + gate * acc
    if final:
        y = _rmsnorm(y, fw_ref[...])
    o_ref[0] = y


def _mlp(x, mod, norm_w, w1_all, w2_all, layer, final_w, final, tm=1024, n_split=4):
    bsz, seq, d = x.shape
    hidden = w1_all.shape[2]
    kern = functools.partial(_mlp_kernel, n_split=n_split, final=final)
    row = lambda v: v.reshape(1, -1).astype(F32)
    layer_block = lambda shape: pl.BlockSpec(shape, lambda b, j: (layer, 0, 0), pipeline_mode=pl.Buffered(1))
    return pl.pallas_call(
        kern,
        grid=(bsz, seq // tm),
        in_specs=[pl.BlockSpec((1, tm, d), lambda b, j: (b, j, 0)),
                  pl.BlockSpec((1, 6, d), lambda b, j: (b, 0, 0)),
                  _const_spec((1, d)), layer_block((1, d, hidden)), layer_block((1, hidden, d)),
                  _const_spec((1, d))],
        out_specs=pl.BlockSpec((1, tm, d), lambda b, j: (b, j, 0)),
        out_shape=jax.ShapeDtypeStruct(x.shape, F32),
        compiler_params=pltpu.CompilerParams(
            dimension_semantics=("arbitrary", "arbitrary"), vmem_limit_bytes=VMEM_LIMIT_BYTES),
        name="mlp_final" if final else "mlp",
    )(x, mod, row(norm_w), w1_all, w2_all, row(final_w))


def kernel(x, c, positions, ada_w, ada_b, norm_mix_w, norm_mlp_w, mlp_w1, mlp_w2, final_norm_w, ev_w_in, lru_conv_w, lru_conv_b, lru_w_a, lru_b_a, lru_w_x, lru_b_x, lru_lambda, ev_w_out, hg_lb_logits, od_w_in, hg_norm_w, gd_conv_w, gd_a_log, gd_dt_bias, gd_norm_w, od_w_out):
    depth = ada_w.shape[0]
    bsz, seq, d = x.shape
    mod = _adaln_mod(c, ada_w, ada_b).reshape(depth, bsz, 6, d)
    w1_all, w2_all = _bf(mlp_w1), _bf(mlp_w2)
    for layer in range(depth):
        j = layer // 2
        if layer % 2 == 0:
            x = _even_mixer(x, mod[layer], positions, norm_mix_w[layer], ev_w_in[j], lru_conv_w[j], lru_conv_b[j],
                            lru_w_a[j], lru_b_a[j], lru_w_x[j], lru_b_x[j], lru_lambda[j], ev_w_out[j])
        else:
            x = _odd_mixer(x, mod[layer], norm_mix_w[layer], hg_lb_logits, layer, od_w_in[j], hg_norm_w[j],
                           gd_conv_w[j], gd_a_log[j], gd_dt_bias[j], gd_norm_w[j], od_w_out[j])
        x = _mlp(x, mod[layer], norm_mlp_w[layer], w1_all, w2_all, layer, final_norm_w,
                 final=(layer == depth - 1))
    return x
```

```python
import functools
import math

import jax
import jax.numpy as jnp
from jax import lax
from jax.experimental import pallas as pl
from jax.experimental.pallas import tpu as pltpu

F32 = jnp.float32
BF16 = jnp.bfloat16

EPS = 1e-6
CONV_WIDTH = 4
LRU_C = 8.0
ROPE_BASE = 10000.0
HEADS = 4
HEAD_DIM = 128
HG_CHUNK = 32
GD_CHUNK = 64
GD_BASE = 16
SUBLANES = 8
LANES = 128
VMEM_LIMIT_BYTES = 56 * 1024 * 1024


def _dot(a, b):
    return jnp.dot(a, b, preferred_element_type=F32)


def _dot_nt(a, b):
    return lax.dot_general(a, b, (((1,), (1,)), ((), ())), preferred_element_type=F32)


def _dot_tn(a, b):
    return lax.dot_general(a, b, (((0,), (0,)), ((), ())), preferred_element_type=F32)


def _bf(x):
    return x.astype(BF16)


def _dot_exact_rhs(sel_bf16, x):
    hi = _bf(x)
    lo = _bf(x - hi.astype(F32))
    width = x.shape[1]
    if width == LANES:
        both = _dot(sel_bf16, jnp.concatenate([hi, lo], axis=1))
        return both[:, :width] + both[:, width:]
    return _dot(sel_bf16, hi) + _dot(sel_bf16, lo)


def _softplus(x):
    return jnp.maximum(x, 0.0) + jnp.log1p(jnp.exp(-jnp.abs(x)))


def _neg_expm1(x):
    t = jnp.tanh(0.5 * x)
    return -2.0 * t / (1.0 - t)


def _silu(x):
    return x * jax.nn.sigmoid(x)


def _rmsnorm(x, w):
    return x * lax.rsqrt(jnp.mean(x * x, axis=-1, keepdims=True) + EPS) * w


def _norm_modulate(x, w, shift, scale):
    return x * lax.rsqrt(jnp.mean(x * x, axis=-1, keepdims=True) + EPS) * (w * (1.0 + scale)) + shift


def _causal_conv(x, prev8, w_ref, bias):
    n, c = x.shape
    x_prev = jnp.concatenate([prev8, x[:n - SUBLANES]], axis=0)
    sub = lax.broadcasted_iota(jnp.int32, (n, c), 0) & (SUBLANES - 1)
    acc = x * w_ref[CONV_WIDTH - 1:CONV_WIDTH, :]
    if bias is not None:
        acc = acc + bias
    for s in range(1, CONV_WIDTH):
        z = jnp.where(sub >= SUBLANES - s, x_prev, x)
        delayed = pltpu.roll(z.reshape(n // SUBLANES, SUBLANES, c), s, axis=1).reshape(n, c)
        acc = acc + delayed * w_ref[CONV_WIDTH - 1 - s:CONV_WIDTH - s, :]
    return acc


def _same_block(n, blk):
    r = lax.broadcasted_iota(jnp.int32, (n, n), 0)
    c = lax.broadcasted_iota(jnp.int32, (n, n), 1)
    sh = int(math.log2(blk))
    return lax.shift_right_logical(r, sh) == lax.shift_right_logical(c, sh)


def _block_masks(n, blk):
    r = lax.broadcasted_iota(jnp.int32, (n, n), 0)
    c = lax.broadcasted_iota(jnp.int32, (n, n), 1)
    same = _same_block(n, blk)
    return same & (c <= r), same & (c < r)


def _unit_lower_inverses(a_mats, n, blk, base):
    r = lax.broadcasted_iota(jnp.int32, (n, n), 0)
    c = lax.broadcasted_iota(jnp.int32, (n, n), 1)
    inner = _same_block(n, base)
    pw = [jnp.where(inner, a, 0.0) for a in a_mats]
    t_inv = [jnp.where(r == c, 1.0, 0.0) - p for p in pw]
    pwb = [_bf(p) for p in pw]
    pw = [_dot(p, p) for p in pwb]
    for _ in range(int(math.log2(base)) - 2):
        pwb = [_bf(p) for p in pw]
        both = [_dot(jnp.concatenate([_bf(t), p], axis=0), p) for t, p in zip(t_inv, pwb)]
        t_inv = [t + b[:n] for t, b in zip(t_inv, both)]
        pw = [b[n:] for b in both]
    t_inv = [t + _dot(_bf(t), _bf(p)) for t, p in zip(t_inv, pw)]
    size = base
    while size < blk:
        size *= 2
        outer = _same_block(n, size)
        off = [_bf(jnp.where(outer & jnp.logical_not(inner), a, 0.0)) for a in a_mats]
        tb = [_bf(t) for t in t_inv]
        t_inv = [t - _dot(b, _bf(_dot(o, b))) for t, b, o in zip(t_inv, tb, off)]
        inner = outer
    return t_inv


def _mod_kernel(c_ref, w_ref, b_ref, o_ref):
    c = c_ref[...]
    o_ref[0] = _dot(_bf(_silu(c)), _bf(w_ref[0])) + b_ref[0]


def _adaln_mod(c, ada_w, ada_b, tn=1536):
    depth, d, n = ada_w.shape
    b = c.shape[0]
    return pl.pallas_call(
        _mod_kernel,
        grid=(depth, n // tn),
        in_specs=[pl.BlockSpec((b, d), lambda l, j: (0, 0)),
                  pl.BlockSpec((1, d, tn), lambda l, j: (l, 0, j)),
                  pl.BlockSpec((1, 1, tn), lambda l, j: (l, 0, j))],
        out_specs=pl.BlockSpec((1, b, tn), lambda l, j: (l, 0, j)),
        out_shape=jax.ShapeDtypeStruct((depth, b, n), F32),
        compiler_params=pltpu.CompilerParams(
            dimension_semantics=("arbitrary", "arbitrary"), vmem_limit_bytes=VMEM_LIMIT_BYTES),
        name="adaln_mod",
    )(c, ada_w, ada_b.reshape(depth, 1, n))


def _lru_scan(a_s, u_s, h0, ts):
    width = a_s.shape[1]
    row = lax.broadcasted_iota(jnp.int32, (SUBLANES, width), 0)

    hprev = h0
    for g in range(ts // SUBLANES):
        rows = slice(g * SUBLANES, (g + 1) * SUBLANES)
        a8 = a_s[rows, :]
        u8 = u_s[rows, :]
        for d in (1, 2, 4):
            keep = row >= d
            a_sh = jnp.where(keep, pltpu.roll(a8, d, axis=0), 1.0)
            u_sh = jnp.where(keep, pltpu.roll(u8, d, axis=0), 0.0)
            u8 = u8 + a8 * u_sh
            a8 = a8 * a_sh
        h8 = u8 + a8 * hprev
        u_s[rows, :] = h8
        hprev = h8[SUBLANES - 1:SUBLANES, :]
    return hprev


def _even_kernel(x_ref, mod_ref, pos_ref, coff_ref, soff_ref, inv_ref, sgn_ref, nw_ref, win_ref, cw_ref,
                 cb_ref, wga_ref, wgx_ref, ba_ref, bx_ref, lam_ref, wout_ref, o_ref,
                 proj_all, a_s, u_s, mix_all, cos_s, sin_s, xprev_s, hprev_s, state_s, *, ts, chunk):
    lw = a_s.shape[1]
    half = lw // 2
    hd = HEAD_DIM
    n_sub = proj_all.shape[0] // ts

    def mix_tile(proj_s, mix_s, cos_view, sin_view):
        xr = proj_s[:, 0:lw]
        xc = _causal_conv(xr, xprev_s[...], cw_ref, cb_ref[...])
        xprev_s[...] = xr[ts - SUBLANES:ts, :]
        xcb = _bf(xc)
        r_pre = jnp.concatenate([_dot(xcb[:, :half], wga_ref[0]), _dot(xcb[:, half:], wga_ref[1])], axis=1)
        i_pre = jnp.concatenate([_dot(xcb[:, :half], wgx_ref[0]), _dot(xcb[:, half:], wgx_ref[1])], axis=1)
        r = jax.nn.sigmoid(r_pre + ba_ref[...])
        i = jax.nn.sigmoid(i_pre + bx_ref[...])
        log_a = (-LRU_C) * r * _softplus(-lam_ref[...])
        a_s[...] = jnp.exp(log_a)
        u_s[...] = jnp.sqrt(_neg_expm1(2.0 * log_a)) * (i * xc)
        hprev_s[...] = _lru_scan(a_s, u_s, hprev_s[...], ts)
        yr = proj_s[:, lw:2 * lw]
        mix_s[:, 0:lw] = _bf(u_s[...] * jax.nn.gelu(yr, approximate=True))

        cos_p, sin_p = cos_view[...], sin_view[...]
        lane = lax.broadcasted_iota(jnp.int32, (ts // 2, hd), 1)
        first = lane < hd // 2
        cos_r, sin_r = pltpu.roll(cos_p, hd // 2, axis=1), pltpu.roll(sin_p, hd // 2, axis=1)
        cos_d = jnp.concatenate([jnp.where(first, cos_p, cos_r), jnp.where(first, cos_r, cos_p)], axis=0)
        sin_d = jnp.concatenate([jnp.where(first, sin_p, sin_r), jnp.where(first, sin_r, sin_p)], axis=0)
        sin_d = sin_d * sgn_ref[...]
        ridx = lax.broadcasted_iota(jnp.int32, (chunk, chunk), 0)
        cidx = lax.broadcasted_iota(jnp.int32, (chunk, chunk), 1)
        rel = (ridx - cidx).astype(F32)
        tcol = lax.broadcasted_iota(jnp.int32, (chunk, 1), 0).astype(F32)
        qoff, koff, voff, goff = 2 * lw, 2 * lw + HEADS * hd, 2 * lw + 2 * HEADS * hd, 2 * lw + 3 * HEADS * hd
        log_gamma = [math.log1p(-(2.0 ** (-5.0 - hh))) for hh in range(HEADS)]
        dmats = [jnp.where(rel >= 0, jnp.exp(lg * jnp.maximum(rel, 0.0)), 0.0) for lg in log_gamma]
        q_decs = [jnp.exp(lg * (tcol + 1.0)) for lg in log_gamma]
        k_decs = [jnp.exp(lg * (chunk - 1.0 - tcol)) for lg in log_gamma]
        n_ret = ts // chunk
        pairs = [(cc, hh) for cc in range(n_ret) for hh in range(HEADS)]
        qb, vb, sc, kv = {}, {}, {}, {}
        for cc, hh in pairs:
            rows = slice(cc * chunk, (cc + 1) * chunk)
            cos_c, sin_c = cos_d[rows], sin_d[rows]
            qc = proj_s[rows, qoff + hh * hd:qoff + (hh + 1) * hd]
            kc = proj_s[rows, koff + hh * hd:koff + (hh + 1) * hd]
            qc = qc * cos_c + pltpu.roll(qc, hd // 2, axis=1) * sin_c
            kc = (kc * cos_c + pltpu.roll(kc, hd // 2, axis=1) * sin_c) * (hd ** -0.5)
            vb[cc, hh] = _bf(proj_s[rows, voff + hh * hd:voff + (hh + 1) * hd])
            qb[cc, hh] = _bf(qc)
            sc[cc, hh] = _bf(_dot_nt(qb[cc, hh], _bf(kc)) * dmats[hh])
            kv[cc, hh] = _dot_tn(_bf(kc * k_decs[hh]), vb[cc, hh])
        st_in = {}
        for hh, lg in enumerate(log_gamma):
            st = state_s[hh]
            for cc in range(n_ret):
                st_in[cc, hh] = _bf(st)
                st = math.exp(lg * chunk) * st + kv[cc, hh]
            state_s[hh] = st
        outs = {p: _dot(sc[p], vb[p]) + _dot(qb[p], st_in[p]) * q_decs[p[1]] for p in pairs}
        for cc, hh in pairs:
            rows = slice(cc * chunk, (cc + 1) * chunk)
            o = outs[cc, hh]
            mu = jnp.mean(o, axis=-1, keepdims=True)
            oc = o - mu
            var = jnp.mean(oc * oc, axis=-1, keepdims=True)
            g = proj_s[rows, goff + hh * hd:goff + (hh + 1) * hd]
            mix_s[rows, lw + hh * hd:lw + (hh + 1) * hd] = _bf(oc * lax.rsqrt(var + EPS) * _silu(g))

    @pl.when(pl.program_id(1) == 0)
    def _():
        xprev_s[...] = jnp.zeros_like(xprev_s)
        hprev_s[...] = jnp.zeros_like(hprev_s)
        state_s[...] = jnp.zeros_like(state_s)

    pos = pos_ref[0]
    base = pos[0:1, 0:1]
    row_i = lax.broadcasted_iota(jnp.int32, pos.shape, 0)
    lane_i = lax.broadcasted_iota(jnp.int32, pos.shape, 1)
    dev = jnp.abs(pos - base - (row_i * LANES + lane_i).astype(F32))
    dev = jnp.max(jnp.max(dev, axis=1, keepdims=True), axis=0, keepdims=True)
    consecutive = dev[0, 0] == 0.0

    @pl.when(consecutive)
    def _():
        ang0 = base * inv_ref[...]
        c0, s0 = jnp.cos(ang0), jnp.sin(ang0)
        cos_s[...] = c0 * coff_ref[...] - s0 * soff_ref[...]
        sin_s[...] = s0 * coff_ref[...] + c0 * soff_ref[...]

    @pl.when(jnp.logical_not(consecutive))
    def _():
        pos_t = pos.T
        first_half = lax.broadcasted_iota(jnp.int32, (ts // 2, hd), 1) < hd // 2
        for t in range(n_sub):
            packed = pl.ds(t * (ts // 2), ts // 2)
            ang = jnp.where(first_half, pos_t[:, 2 * t:2 * t + 1], pos_t[:, 2 * t + 1:2 * t + 2]) * inv_ref[...]
            cos_s[packed, :] = jnp.cos(ang)
            sin_s[packed, :] = jnp.sin(ang)

    x = x_ref[0]
    shift, scale, gate = mod_ref[0, 0:1, :], mod_ref[0, 1:2, :], mod_ref[0, 2:3, :]
    proj_all[...] = _dot(_bf(_norm_modulate(x, nw_ref[...], shift, scale)), win_ref[...])
    for t in range(n_sub):
        rows = pl.ds(t * ts, ts)
        packed = pl.ds(t * (ts // 2), ts // 2)
        mix_tile(proj_all.at[rows], mix_all.at[rows], cos_s.at[packed], sin_s.at[packed])
    o_ref[0] = x + gate * _dot(mix_all[...], wout_ref[...])


def _blockdiag_halves(w):
    n, bi, bj = w.shape
    eye = jnp.eye(n // 2, dtype=w.dtype)
    halves = []
    for p in range(2):
        blocks = w[p * (n // 2):(p + 1) * (n // 2)]
        dense = jnp.einsum('nm,nij->nimj', eye, blocks).reshape(n // 2 * bi, n // 2 * bj)
        halves.append(dense)
    return jnp.stack(halves)


def _const_spec(shape):
    nd = len(shape)
    return pl.BlockSpec(shape, lambda *_: (0,) * nd, pipeline_mode=pl.Buffered(1))


def _pack_half_tiles(v, ts):
    lead = v.shape[:-1]
    p = v.reshape(lead + (v.shape[-1] // ts, 2, ts // 2))
    p = jnp.swapaxes(p, -1, -2)
    p = jnp.repeat(p, HEAD_DIM // 2, axis=-1)
    return p.reshape(lead + (v.shape[-1] // 2, HEAD_DIM))


def _even_mixer(x, mod, positions, norm_w, w_in, conv_w, conv_b, w_a, b_a, w_x, b_x, lam, w_out,
                ts=256, n_sub=4, chunk=128):
    bsz, seq, d = x.shape
    lw = lam.shape[-1]
    n_in = w_in.shape[1]
    half = HEAD_DIM // 2
    inv = ROPE_BASE ** (-jnp.arange(half, dtype=F32) / half)
    inv2 = jnp.concatenate([inv, inv]).reshape(1, HEAD_DIM)
    sgn = jnp.concatenate([-jnp.ones((half,), F32), jnp.ones((half,), F32)]).reshape(1, HEAD_DIM)
    row = lambda v: v.reshape(1, -1).astype(F32)
    rows = n_sub * ts
    assert ts == 2 * LANES, "the packed rotary layout pairs the two 128-row halves of a tile"
    pos_rows = positions.astype(F32).reshape(bsz, seq // LANES, LANES)
    off_packed = _pack_half_tiles(jnp.arange(rows, dtype=F32), ts)
    cos_off, sin_off = jnp.cos(off_packed * inv2), jnp.sin(off_packed * inv2)
    kern = functools.partial(_even_kernel, ts=ts, chunk=chunk)
    return pl.pallas_call(
        kern,
        grid=(bsz, seq // rows),
        in_specs=[pl.BlockSpec((1, rows, d), lambda b, j: (b, j, 0)),
                  pl.BlockSpec((1, 6, d), lambda b, j: (b, 0, 0)),
                  pl.BlockSpec((1, rows // LANES, LANES), lambda b, j: (b, j, 0)),
                  _const_spec((rows // 2, HEAD_DIM)), _const_spec((rows // 2, HEAD_DIM)),
                  _const_spec((1, HEAD_DIM)), _const_spec((1, HEAD_DIM)), _const_spec((1, d)),
                  _const_spec((d, n_in)), _const_spec((CONV_WIDTH, lw)), _const_spec((1, lw)),
                  _const_spec((2, lw // 2, lw // 2)), _const_spec((2, lw // 2, lw // 2)),
                  _const_spec((1, lw)), _const_spec((1, lw)), _const_spec((1, lw)),
                  _const_spec((lw + HEADS * HEAD_DIM, d))],
        out_specs=pl.BlockSpec((1, rows, d), lambda b, j: (b, j, 0)),
        out_shape=jax.ShapeDtypeStruct(x.shape, F32),
        scratch_shapes=[pltpu.VMEM((rows, n_in), F32), pltpu.VMEM((ts, lw), F32), pltpu.VMEM((ts, lw), F32),
                        pltpu.VMEM((rows, lw + HEADS * HEAD_DIM), BF16),
                        pltpu.VMEM((rows // 2, HEAD_DIM), F32), pltpu.VMEM((rows // 2, HEAD_DIM), F32),
                        pltpu.VMEM((SUBLANES, lw), F32),
                        pltpu.VMEM((1, lw), F32), pltpu.VMEM((HEADS, HEAD_DIM, HEAD_DIM), F32)],
        compiler_params=pltpu.CompilerParams(
            dimension_semantics=("arbitrary", "arbitrary"), vmem_limit_bytes=VMEM_LIMIT_BYTES),
        name="even_mixer",
    )(x, mod, pos_rows, cos_off, sin_off, inv2, sgn, row(norm_w), _bf(w_in), conv_w.astype(F32), row(conv_b),
      _bf(_blockdiag_halves(w_a)), _bf(_blockdiag_halves(w_x)), row(b_a), row(b_x), row(lam), _bf(w_out))


def _odd_kernel(x_ref, mod_ref, nw_ref, win_ref, wtail_ref, lbl_ref, hgw_ref, cw_ref, alog_ref, dtb_ref, gdw_ref,
                wout_ref, o_ref, proj_all, b_s, qi_s, ki_s, mix_all, cprev_s, hg_state_s, gd_state_s,
                *, ts, layer):
    hd = HEAD_DIM
    hw = HEADS * hd
    n_sub = proj_all.shape[0] // ts

    def mix_tile(proj_s, mix_s):
        logits = lbl_ref[...]
        e = jnp.exp(logits - jnp.max(logits, axis=0, keepdims=True))
        lb = jnp.sum(e[1:layer + 1], axis=0, keepdims=True) / jnp.sum(e, axis=0, keepdims=True)
        key = (1.0 - lb) * jax.nn.sigmoid(-proj_s[:, hw:2 * hw])
        log_f = jnp.log1p(-key)
        incl32, _ = _block_masks(ts, HG_CHUNK)
        b_s[...] = _dot_exact_rhs(jnp.where(incl32, 1.0, 0.0).astype(BF16), log_f)
        mid = HG_CHUNK // 2
        n_hg = ts // HG_CHUNK
        for cc in range(n_hg):
            rows = slice(cc * HG_CHUNK, (cc + 1) * HG_CHUNK)
            b = b_s[rows, :]
            b_mid = b[mid:mid + 1, :]
            qi_s[rows, :] = _bf(proj_s[rows, 0:hw] * jnp.exp(b - b_mid))
            ki_s[rows, :] = _bf(key[rows] * jnp.exp(b_mid - b))
        head_cols = [slice(hh * hd, (hh + 1) * hd) for hh in range(HEADS)]
        hg_v = [_bf(proj_s[:, 2 * hw + hh * hd:2 * hw + (hh + 1) * hd]) for hh in range(HEADS)]
        hg_scores = [jnp.where(incl32, _dot_nt(qi_s[:, cols], ki_s[:, cols]), 0.0) for cols in head_cols]
        hg_intra = [_dot(_bf(sc), v) for sc, v in zip(hg_scores, hg_v)]
        hg_pairs = [(cc, hh) for cc in range(n_hg) for hh in range(HEADS)]
        hg_rows = [slice(cc * HG_CHUNK, (cc + 1) * HG_CHUNK) for cc in range(n_hg)]
        hg_kv, hg_dec, hg_qs = {}, {}, {}
        for cc, hh in hg_pairs:
            rows, cols = hg_rows[cc], head_cols[hh]
            b = b_s[rows, cols]
            b_last = b[HG_CHUNK - 1:HG_CHUNK, :]
            hg_dec[cc, hh] = jnp.exp(b_last)
            hg_qs[cc, hh] = _bf(proj_s[rows, cols] * jnp.exp(b))
            hg_kv[cc, hh] = _dot_tn(hg_v[hh][rows], _bf(key[rows, cols] * jnp.exp(b_last - b)))
        hg_in = {}
        for hh in range(HEADS):
            st_t = hg_state_s[hh]
            for cc in range(n_hg):
                hg_in[cc, hh] = _bf(st_t)
                st_t = st_t * hg_dec[cc, hh] + hg_kv[cc, hh]
            hg_state_s[hh] = st_t
        hg_out = {p: hg_intra[p[1]][hg_rows[p[0]]] + _dot_nt(hg_qs[p], hg_in[p]) for p in hg_pairs}
        for cc, hh in hg_pairs:
            rows = hg_rows[cc]
            o = hg_out[cc, hh]
            on = o * lax.rsqrt(jnp.mean(o * o, axis=-1, keepdims=True) + EPS) * hgw_ref[...]
            gh = proj_s[rows, 3 * hw + hh * hd:3 * hw + (hh + 1) * hd]
            mix_s[rows, head_cols[hh]] = _bf(on * _silu(gh))

        qkv_raw = proj_s[:, 4 * hw:7 * hw]
        qkv = _silu(_causal_conv(qkv_raw, cprev_s[...], cw_ref, None))
        cprev_s[...] = qkv_raw[ts - SUBLANES:ts, :]
        ab = proj_s[:, 8 * hw:8 * hw + LANES]
        g_all = -jnp.exp(alog_ref[...]) * _softplus(ab + dtb_ref[...])
        beta_all = jax.nn.sigmoid(ab)
        incl64, strict64 = _block_masks(ts, GD_CHUNK)
        gcum = _dot_exact_rhs(jnp.where(incl64, 1.0, 0.0).astype(BF16), g_all)
        gcum_t = gcum.T
        n_gd = ts // GD_CHUNK
        gd_q, gd_k, gd_gcol, gd_qk, a_mats, rhs_all = [], [], [], [], [], []
        for hh in range(HEADS):
            qh = qkv[:, hh * hd:(hh + 1) * hd]
            kh = qkv[:, hw + hh * hd:hw + (hh + 1) * hd]
            vh = qkv[:, 2 * hw + hh * hd:2 * hw + (hh + 1) * hd]
            qh = qh * lax.rsqrt(jnp.sum(qh * qh, axis=-1, keepdims=True) + EPS) * (hd ** -0.5)
            kh = kh * lax.rsqrt(jnp.sum(kh * kh, axis=-1, keepdims=True) + EPS)
            gcol = gcum[:, hh:hh + 1]
            grow = gcum_t[hh:hh + 1, :]
            beta = beta_all[:, HEADS + hh:HEADS + hh + 1]
            decay = jnp.where(incl64, jnp.exp(jnp.where(incl64, gcol - grow, 0.0)), 0.0)
            kb = kh * beta
            khb = _bf(kh)
            kq = _dot_nt(jnp.concatenate([_bf(kb), _bf(qh)], axis=0), khb)
            a_mats.append(jnp.where(strict64, kq[:ts] * decay, 0.0))
            gd_qk.append(_bf(jnp.where(incl64, kq[ts:] * decay, 0.0)))
            egc = jnp.exp(gcol)
            rhs_all.append(_bf(jnp.concatenate([vh * beta, kb * egc], axis=1)))
            gd_q.append(_bf(qh * egc))
            gd_k.append(kh)
            gd_gcol.append(gcol)
        t_inv = _unit_lower_inverses(a_mats, ts, GD_CHUNK, GD_BASE)
        sol = [_dot(_bf(t), rh) for t, rh in zip(t_inv, rhs_all)]
        chunk_rows = [slice(cc * GD_CHUNK, (cc + 1) * GD_CHUNK) for cc in range(n_gd)]
        ktu = [[None] * n_gd for _ in range(HEADS)]
        ktw = [[None] * n_gd for _ in range(HEADS)]
        g_last = [[None] * n_gd for _ in range(HEADS)]
        for cc, rows in enumerate(chunk_rows):
            for hh in range(HEADS):
                gcol = gd_gcol[hh]
                g_last[hh][cc] = gcol[(cc + 1) * GD_CHUNK - 1:(cc + 1) * GD_CHUNK, :]
                k_g = _bf(gd_k[hh][rows] * jnp.exp(g_last[hh][cc] - gcol[rows]))
                kt = _dot_tn(k_g, _bf(sol[hh][rows]))
                ktu[hh][cc] = kt[:, :hd]
                ktw[hh][cc] = _bf(kt[:, hd:])
        st = [gd_state_s[hh] for hh in range(HEADS)]
        wq = {}
        for cc in range(n_gd):
            for hh in range(HEADS):
                rows = chunk_rows[cc]
                lhs = jnp.concatenate([ktw[hh][cc], _bf(sol[hh][rows, hd:]), gd_q[hh][rows]], axis=0)
                prod = _dot(lhs, _bf(st[hh]))
                st[hh] = st[hh] * jnp.exp(g_last[hh][cc]) + ktu[hh][cc] - prod[:hd]
                wq[cc, hh] = prod[hd:]
        for hh in range(HEADS):
            gd_state_s[hh] = st[hh]
        gd_pairs = [(cc, hh) for cc in range(n_gd) for hh in range(HEADS)]
        v_newb = {(cc, hh): _bf(sol[hh][chunk_rows[cc], :hd] - wq[cc, hh][:GD_CHUNK]) for cc, hh in gd_pairs}
        gd_out = {(cc, hh): wq[cc, hh][GD_CHUNK:] + _dot(gd_qk[hh][chunk_rows[cc], chunk_rows[cc]], v_newb[cc, hh])
                  for cc, hh in gd_pairs}
        for cc, hh in gd_pairs:
            rows = chunk_rows[cc]
            o = gd_out[cc, hh]
            on = o * lax.rsqrt(jnp.mean(o * o, axis=-1, keepdims=True) + EPS) * gdw_ref[...]
            zh = proj_s[rows, 7 * hw + hh * hd:7 * hw + (hh + 1) * hd]
            mix_s[rows, hw + hh * hd:hw + (hh + 1) * hd] = _bf(on * _silu(zh))

    @pl.when(pl.program_id(1) == 0)
    def _():
        cprev_s[...] = jnp.zeros_like(cprev_s)
        hg_state_s[...] = jnp.zeros_like(hg_state_s)
        gd_state_s[...] = jnp.zeros_like(gd_state_s)

    x = x_ref[0]
    shift, scale, gate = mod_ref[0, 0:1, :], mod_ref[0, 1:2, :], mod_ref[0, 2:3, :]
    hb = _bf(_norm_modulate(x, nw_ref[...], shift, scale))
    proj_all[:, 0:8 * hw] = _dot(hb, win_ref[...])
    proj_all[:, 8 * hw:] = _dot(hb, wtail_ref[...])
    for t in range(n_sub):
        rows = pl.ds(t * ts, ts)
        mix_tile(proj_all.at[rows], mix_all.at[rows])
    o_ref[0] = x + gate * _dot(mix_all[...], wout_ref[...])


def _odd_mixer(x, mod, norm_w, lb_logits, layer, w_in, hg_norm_w, conv_w, a_log, dt_bias, gd_norm_w, w_out,
               ts=256, n_sub=4):
    bsz, seq, d = x.shape
    hw = HEADS * HEAD_DIM
    n_in = 8 * hw + LANES
    w_main = _bf(w_in[:, :8 * hw])
    w_tail = jnp.zeros((d, LANES), BF16).at[:, :w_in.shape[1] - 8 * hw].set(_bf(w_in[:, 8 * hw:]))
    lane_pad = lambda v, off: jnp.zeros((1, LANES), F32).at[0, off:off + v.shape[0]].set(v.astype(F32))
    row = lambda v: v.reshape(1, -1).astype(F32)
    depth = lb_logits.shape[0]
    rows = n_sub * ts
    kern = functools.partial(_odd_kernel, ts=ts, layer=layer)
    return pl.pallas_call(
        kern,
        grid=(bsz, seq // rows),
        in_specs=[pl.BlockSpec((1, rows, d), lambda b, j: (b, j, 0)),
                  pl.BlockSpec((1, 6, d), lambda b, j: (b, 0, 0)),
                  _const_spec((1, d)), _const_spec((d, 8 * hw)), _const_spec((d, LANES)), _const_spec((depth, hw)),
                  _const_spec((1, HEAD_DIM)), _const_spec((CONV_WIDTH, 3 * hw)),
                  _const_spec((1, LANES)), _const_spec((1, LANES)), _const_spec((1, HEAD_DIM)),
                  _const_spec((2 * hw, d))],
        out_specs=pl.BlockSpec((1, rows, d), lambda b, j: (b, j, 0)),
        out_shape=jax.ShapeDtypeStruct(x.shape, F32),
        scratch_shapes=[pltpu.VMEM((rows, n_in), F32), pltpu.VMEM((ts, hw), F32),
                        pltpu.VMEM((ts, hw), BF16), pltpu.VMEM((ts, hw), BF16),
                        pltpu.VMEM((rows, 2 * hw), BF16), pltpu.VMEM((SUBLANES, 3 * hw), F32),
                        pltpu.VMEM((HEADS, HEAD_DIM, HEAD_DIM), F32), pltpu.VMEM((HEADS, HEAD_DIM, HEAD_DIM), F32)],
        compiler_params=pltpu.CompilerParams(
            dimension_semantics=("arbitrary", "arbitrary"), vmem_limit_bytes=VMEM_LIMIT_BYTES),
        name="odd_mixer",
    )(x, mod, row(norm_w), w_main, w_tail, lb_logits.astype(F32), row(hg_norm_w), conv_w.astype(F32),
      lane_pad(a_log, 0), lane_pad(dt_bias, 0), row(gd_norm_w), _bf(w_out))


def _mlp_kernel(x_ref, mod_ref, nw_ref, w1_ref, w2_ref, fw_ref, o_ref, *, n_split, final):
    x = x_ref[0]
    shift, scale, gate = mod_ref[0, 3:4, :], mod_ref[0, 4:5, :], mod_ref[0, 5:6, :]
    h = _bf(_norm_modulate(x, nw_ref[...], shift, scale))
    hidden = w1_ref.shape[2]
    step = hidden // n_split
    acc = jnp.zeros(x.shape, F32)
    for s in range(n_split):
        hid = jnp.square(jnp.maximum(_dot(h, w1_ref[0, :, s * step:(s + 1) * step]), 0.0))
        acc = acc + _dot(_bf(hid), w2_ref[0, s * step:(s + 1) * step, :])
    y = x + gate * acc
    if final:
        y = _rmsnorm(y, fw_ref[...])
    o_ref[0] = y


def _mlp(x, mod, norm_w, w1_all, w2_all, layer, final_w, final, tm=1024, n_split=4):
    bsz, seq, d = x.shape
    hidden = w1_all.shape[2]
    kern = functools.partial(_mlp_kernel, n_split=n_split, final=final)
    row = lambda v: v.reshape(1, -1).astype(F32)
    layer_block = lambda shape: pl.BlockSpec(shape, lambda b, j: (layer, 0, 0), pipeline_mode=pl.Buffered(1))
    return pl.pallas_call(
        kern,
        grid=(bsz, seq // tm),
        in_specs=[pl.BlockSpec((1, tm, d), lambda b, j: (b, j, 0)),
                  pl.BlockSpec((1, 6, d), lambda b, j: (b, 0, 0)),
                  _const_spec((1, d)), layer_block((1, d, hidden)), layer_block((1, hidden, d)),
                  _const_spec((1, d))],
        out_specs=pl.BlockSpec((1, tm, d), lambda b, j: (b, j, 0)),
        out_shape=jax.ShapeDtypeStruct(x.shape, F32),
        compiler_params=pltpu.CompilerParams(
            dimension_semantics=("arbitrary", "arbitrary"), vmem_limit_bytes=VMEM_LIMIT_BYTES),
        name="mlp_final" if final else "mlp",
    )(x, mod, row(norm_w), w1_all, w2_all, row(final_w))


def kernel(x, c, positions, ada_w, ada_b, norm_mix_w, norm_mlp_w, mlp_w1, mlp_w2, final_norm_w, ev_w_in, lru_conv_w, lru_conv_b, lru_w_a, lru_b_a, lru_w_x, lru_b_x, lru_lambda, ev_w_out, hg_lb_logits, od_w_in, hg_norm_w, gd_conv_w, gd_a_log, gd_dt_bias, gd_norm_w, od_w_out):
    depth = ada_w.shape[0]
    bsz, seq, d = x.shape
    mod = _adaln_mod(c, ada_w, ada_b).reshape(depth, bsz, 6, d)
    w1_all, w2_all = _bf(mlp_w1), _bf(mlp_w2)
    for layer in range(depth):
        j = layer // 2
        if layer % 2 == 0:
            x = _even_mixer(x, mod[layer], positions, norm_mix_w[layer], ev_w_in[j], lru_conv_w[j], lru_conv_b[j],
                            lru_w_a[j], lru_b_a[j], lru_w_x[j], lru_b_x[j], lru_lambda[j], ev_w_out[j])
        else:
            x = _odd_mixer(x, mod[layer], norm_mix_w[layer], hg_lb_logits, layer, od_w_in[j], hg_norm_w[j],
                           gd_conv_w[j], gd_a_log[j], gd_dt_bias[j], gd_norm_w[j], od_w_out[j])
        x = _mlp(x, mod[layer], norm_mlp_w[layer], w1_all, w2_all, layer, final_norm_w,
                 final=(layer == depth - 1))
    return x
```
